```python
import jax, jax.numpy as jnp
from jax import lax
import numpy as np

D_MODEL = 1024
BATCH = 16
SEQ = 2048
DEPTH = 4
DEC_BATCH = 2
DEC_SEQ = 16384
PAST_LEN = 128

GLA_HEADS = 4
GLA_WIDTH = D_MODEL // 2
GLA_DV = GLA_WIDTH // GLA_HEADS
GLA_DK = GLA_DV // 2
GLA_KWIDTH = GLA_HEADS * GLA_DK
GLA_GATE_RANK = 16
GLA_GATE_TAU = 16.0
GLA_CHUNK = 64
SGU_HEADS = 4
SGU_WIDTH = D_MODEL // 4
SGU_HEAD_DIM = SGU_WIDTH // SGU_HEADS
SGU_CHUNK = 128
POOL_WINDOWS = (2, 4, 8, 16)
POOL_GROUPS = len(POOL_WINDOWS)
POOL_WIDTH = D_MODEL // 4
POOL_GROUP_DIM = POOL_WIDTH // POOL_GROUPS
MIX_WIDTH = GLA_WIDTH + SGU_WIDTH + POOL_WIDTH
D_FF = ((8 * D_MODEL // 3 + 127) // 128) * 128
EPS = 1e-6
IN_PARTS = (GLA_KWIDTH, GLA_KWIDTH, GLA_WIDTH, GLA_WIDTH, 2 * GLA_GATE_RANK,
            SGU_WIDTH, SGU_WIDTH, POOL_WIDTH)
IN_WIDTH = sum(IN_PARTS)
IN_SPLITS = tuple(int(s) for s in np.cumsum(IN_PARTS)[:-1])

kernel_name = "hybrid_gla_sgu_pool_encoder"


def rmsnorm(x, g):
    xf = x.astype(jnp.float32)
    y = xf * lax.rsqrt(jnp.mean(xf * xf, axis=-1, keepdims=True) + EPS)
    return (y * g).astype(x.dtype)


def layernorm(x, g, b):
    xf = x.astype(jnp.float32)
    mu = jnp.mean(xf, axis=-1, keepdims=True)
    xc = xf - mu
    y = xc * lax.rsqrt(jnp.mean(xc * xc, axis=-1, keepdims=True) + EPS)
    return (y * g + b).astype(x.dtype)


def gla_direction(q, k, v, log_a):
    B, S, H, DK = q.shape
    DV = v.shape[-1]
    C = GLA_CHUNK
    N = S // C

    def to_chunks(t):
        return t.reshape(B, N, C, H, t.shape[-1]).transpose(0, 3, 1, 2, 4)

    q, k, v, log_a = to_chunks(q), to_chunks(k), to_chunks(v), to_chunks(log_a)
    G = jnp.cumsum(log_a, axis=3)
    G_last = G[:, :, :, -1:, :]
    q_dec = q * jnp.exp(G)
    k_dec = k * jnp.exp(-G)
    mask = jnp.tril(jnp.ones((C, C), dtype=bool))
    A = jnp.where(mask, jnp.einsum('bhnid,bhnjd->bhnij', q_dec, k_dec), 0.0)
    o = jnp.einsum('bhnij,bhnjv->bhniv', A, v)
    kv = jnp.einsum('bhnjd,bhnjv->bhndv', k * jnp.exp(G_last - G), v)
    chunk_decay = jnp.exp(G_last[:, :, :, 0, :])

    def step(state, inp):
        dec, kv_n = inp
        return dec[..., None] * state + kv_n, state

    init = jnp.zeros((B, H, DK, DV), jnp.float32)
    _, prev = lax.scan(step, init, (jnp.moveaxis(chunk_decay, 2, 0), jnp.moveaxis(kv, 2, 0)))
    prev = jnp.moveaxis(prev, 0, 2)
    o = o + jnp.einsum('bhnid,bhndv->bhniv', q_dec, prev)
    return o.transpose(0, 2, 3, 1, 4).reshape(B, S, H, DV)


def gla_mixer(q, k, v, g, lr, w2, b2, norm_g):
    B, S, _ = q.shape
    dt = q.dtype
    qf = q.astype(jnp.float32).reshape(B, S, GLA_HEADS, GLA_DK) * (GLA_DK ** -0.5)
    kf = k.astype(jnp.float32).reshape(B, S, GLA_HEADS, GLA_DK)
    vf = v.astype(jnp.float32).reshape(B, S, GLA_HEADS, GLA_DV)
    lr = lr.astype(jnp.float32).reshape(B, S, 2, GLA_GATE_RANK)
    z = jnp.einsum('bsdr,drk->bsdk', lr, w2.astype(jnp.float32)) + b2.astype(jnp.float32)
    log_a = (jax.nn.log_sigmoid(z) / GLA_GATE_TAU).reshape(B, S, 2, GLA_HEADS, GLA_DK)
    o_f = gla_direction(qf, kf, vf, log_a[:, :, 0])
    flip = lambda t: jnp.flip(t, axis=1)
    o_b = flip(gla_direction(flip(qf), flip(kf), flip(vf), flip(log_a[:, :, 1])))
    o = o_f + o_b
    o = o * lax.rsqrt(jnp.mean(o * o, axis=-1, keepdims=True) + EPS)
    o = o.reshape(B, S, GLA_WIDTH) * norm_g * jax.nn.silu(g.astype(jnp.float32))
    return o.astype(dt)


def spatial_gating(u, v, ln_g, ln_b, w_s, b_s):
    B, S, _ = u.shape
    N = S // SGU_CHUNK
    v = layernorm(v, ln_g, ln_b)
    vh = v.reshape(B, N, SGU_CHUNK, SGU_HEADS, SGU_HEAD_DIM)
    mixed = jnp.einsum('hts,bnshc->bnthc', w_s, vh) + jnp.transpose(b_s)[None, None, :, :, None]
    return (u * mixed.reshape(B, S, SGU_WIDTH)).astype(u.dtype)


def pool_mixer(xp, pool_w, pool_scale):
    B, S, _ = xp.shape
    xf = xp.astype(jnp.float32).reshape(B, S, POOL_GROUPS, POOL_GROUP_DIM)
    cs = jnp.concatenate([jnp.zeros((B, 1, POOL_GROUPS, POOL_GROUP_DIM), jnp.float32),
                          jnp.cumsum(xf, axis=1)], axis=1)
    t = jnp.arange(S)
    pooled = []
    for gi, w in enumerate(POOL_WINDOWS):
        h = w // 2
        lo = jnp.clip(t - h, 0, S)
        hi = jnp.clip(t + h, 0, S)
        cnt = (hi - lo).astype(jnp.float32)[None, :, None]
        c = cs[:, :, gi]
        pooled.append((jnp.take(c, hi, axis=1) - jnp.take(c, lo, axis=1)) / cnt)
    pooled = jnp.stack(pooled, axis=2)
    d = pooled - xf
    y = jnp.einsum('bsgc,gcd->bsgd', d, pool_w.astype(jnp.float32)).reshape(B, S, POOL_WIDTH)
    return (y * pool_scale).astype(xp.dtype)


def conv_glu_ffn(h, w1, conv_w, conv_b, w2):
    a, up = jnp.split(h @ w1, 2, axis=-1)
    ap = jnp.pad(a, ((0, 0), (1, 1), (0, 0)))
    a = ap[:, :-2] * conv_w[0] + ap[:, 1:-1] * conv_w[1] + ap[:, 2:] * conv_w[2] + conv_b
    return (jax.nn.silu(a) * up) @ w2


def trunk(x, norm1_g, w_in, gla_gate_w2, gla_gate_b, gla_norm_g, sgu_ln_g, sgu_ln_b,
          sgu_w, sgu_b, pool_w, pool_scale, w_o, norm2_g, w_ffn_in, conv_w, conv_b,
          w_ffn_out, norm_f):
    for l in range(DEPTH):
        h = rmsnorm(x, norm1_g[l])
        p = h @ w_in[l]
        q, k, v, g, lr, u, vs, xp = jnp.split(p, IN_SPLITS, axis=-1)
        a_out = gla_mixer(q, k, v, g, lr, gla_gate_w2[l], gla_gate_b[l], gla_norm_g[l])
        b_out = spatial_gating(jax.nn.gelu(u), jax.nn.gelu(vs), sgu_ln_g[l], sgu_ln_b[l],
                               sgu_w[l], sgu_b[l])
        c_out = pool_mixer(xp, pool_w[l], pool_scale[l])
        x = x + jnp.concatenate([a_out, b_out, c_out], axis=-1) @ w_o[l]
        h = rmsnorm(x, norm2_g[l])
        x = x + conv_glu_ffn(h, w_ffn_in[l], conv_w[l], conv_b[l], w_ffn_out[l])
    return rmsnorm(x, norm_f)


def setup_inputs(seed: int = 0) -> dict:
    key = jax.random.key(seed)
    ks = jax.random.split(key, 24)
    f32 = jnp.float32
    nrm = lambda k, shape, s: jax.random.normal(k, shape, f32) * s
    return {
        "x_prompt": jax.random.normal(ks[0], (BATCH, SEQ, D_MODEL), f32),
        "x_sample": jax.random.normal(ks[1], (DEC_BATCH, DEC_SEQ, D_MODEL), f32),
        "norm1_g": 1.0 + nrm(ks[2], (DEPTH, D_MODEL), 0.02),
        "w_in": nrm(ks[3], (DEPTH, D_MODEL, IN_WIDTH), D_MODEL ** -0.5),
        "gla_gate_w2": nrm(ks[4], (DEPTH, 2, GLA_GATE_RANK, GLA_KWIDTH), GLA_GATE_RANK ** -0.5),
        "gla_gate_b": nrm(ks[5], (DEPTH, 2, GLA_KWIDTH), 0.1),
        "gla_norm_g": 1.0 + nrm(ks[6], (DEPTH, GLA_WIDTH), 0.02),
        "sgu_ln_g": 1.0 + nrm(ks[7], (DEPTH, SGU_WIDTH), 0.02),
        "sgu_ln_b": nrm(ks[8], (DEPTH, SGU_WIDTH), 0.02),
        "sgu_w": nrm(ks[9], (DEPTH, SGU_HEADS, SGU_CHUNK, SGU_CHUNK), SGU_CHUNK ** -0.5),
        "sgu_b": 1.0 + nrm(ks[10], (DEPTH, SGU_HEADS, SGU_CHUNK), 0.02),
        "pool_w": nrm(ks[11], (DEPTH, POOL_GROUPS, POOL_GROUP_DIM, POOL_GROUP_DIM), POOL_GROUP_DIM ** -0.5),
        "pool_scale": 1.0 + nrm(ks[12], (DEPTH, POOL_WIDTH), 0.1),
        "w_o": nrm(ks[13], (DEPTH, MIX_WIDTH, D_MODEL), MIX_WIDTH ** -0.5),
        "norm2_g": 1.0 + nrm(ks[14], (DEPTH, D_MODEL), 0.02),
        "w_ffn_in": nrm(ks[15], (DEPTH, D_MODEL, 2 * D_FF), D_MODEL ** -0.5),
        "conv_w": nrm(ks[16], (DEPTH, 3, D_FF), 3 ** -0.5),
        "conv_b": nrm(ks[17], (DEPTH, D_FF), 0.02),
        "w_ffn_out": nrm(ks[18], (DEPTH, D_FF, D_MODEL), D_FF ** -0.5),
        "norm_f": 1.0 + nrm(ks[19], (D_MODEL,), 0.02),
    }


def reference(x_prompt, x_sample, norm1_g, w_in, gla_gate_w2, gla_gate_b, gla_norm_g,
              sgu_ln_g, sgu_ln_b, sgu_w, sgu_b, pool_w, pool_scale, w_o, norm2_g,
              w_ffn_in, conv_w, conv_b, w_ffn_out, norm_f):
    y_prompt = trunk(x_prompt, norm1_g, w_in, gla_gate_w2, gla_gate_b, gla_norm_g,
                     sgu_ln_g, sgu_ln_b, sgu_w, sgu_b, pool_w, pool_scale, w_o, norm2_g,
                     w_ffn_in, conv_w, conv_b, w_ffn_out, norm_f)
    y_sample = trunk(x_sample, norm1_g, w_in, gla_gate_w2, gla_gate_b, gla_norm_g,
                     sgu_ln_g, sgu_ln_b, sgu_w, sgu_b, pool_w, pool_scale, w_o, norm2_g,
                     w_ffn_in, conv_w, conv_b, w_ffn_out, norm_f)
    return (y_prompt, y_sample)
```

```python
import functools

import jax
import jax.numpy as jnp
from jax import lax
from jax.experimental import pallas as pl
from jax.experimental.pallas import tpu as pltpu

F32 = jnp.float32
BF16 = jnp.bfloat16

D_MODEL = 1024
DEPTH = 4
GLA_HEADS = 4
GLA_WIDTH = 512
GLA_DV = 128
GLA_DK = 64
GLA_KWIDTH = 256
GLA_GATE_RANK = 16
GLA_GATE_TAU = 16.0
GLA_CHUNK = 64
SGU_HEADS = 4
SGU_WIDTH = 256
SGU_HEAD_DIM = 64
SGU_CHUNK = 128
POOL_WINDOWS = (2, 4, 8, 16)
POOL_WIDTH = 256
POOL_GROUP_DIM = 64
POOL_HALO = max(POOL_WINDOWS) // 2
D_FF = 2816
EPS = 1e-6

V7X_LANES = 128
V7X_SUBLANES = 8
V7X_VMEM_BYTES = 64 * 1024 * 1024

LR_PAD = V7X_LANES
A_WIDTH = 2 * GLA_KWIDTH + GLA_WIDTH + LR_PAD
B_WIDTH = GLA_WIDTH + 2 * SGU_WIDTH + POOL_WIDTH
A_Q, A_K, A_V, A_LR = 0, GLA_KWIDTH, 2 * GLA_KWIDTH, 2 * GLA_KWIDTH + GLA_WIDTH
B_G, B_U, B_VS, B_XP = 0, GLA_WIDTH, GLA_WIDTH + SGU_WIDTH, GLA_WIDTH + 2 * SGU_WIDTH

HALO = V7X_SUBLANES

TM_PROJ = 512
TB_GLA = 256
TM_MIX = 512
TM_FFN = 512
FFN_COL_CHUNK = 256

VMEM_LIMIT_PROJ = 40 * 1024 * 1024
VMEM_LIMIT_GLA = 32 * 1024 * 1024
VMEM_LIMIT_MIX = 48 * 1024 * 1024
VMEM_LIMIT_FFN = 56 * 1024 * 1024


def _rmsnorm(x, g):
    return x * lax.rsqrt(jnp.mean(x * x, axis=-1, keepdims=True) + EPS) * g


def _dot(a, b):
    return jnp.dot(a, b, preferred_element_type=F32)


def _dot_nt(a, b):
    return lax.dot_general(a, b, (((1,), (1,)), ((), ())), preferred_element_type=F32)


def _dot_tn(a, b):
    return lax.dot_general(a, b, (((0,), (0,)), ((), ())), preferred_element_type=F32)


def _inproj_kernel(x_ref, g_ref, w_ref, a_ref, b_ref):
    hb = _rmsnorm(x_ref[...], g_ref[...]).astype(BF16)
    a_ref[...] = _dot(hb, w_ref[:, :A_WIDTH])
    b_ref[...] = _dot(hb, w_ref[:, A_WIDTH:])


def _inproj(x, norm_g, w_in, layer):
    bsz, seq, _ = x.shape
    tm = TM_PROJ
    layer_block = lambda b, j: (layer, 0, 0)
    return pl.pallas_call(
        _inproj_kernel,
        grid=(bsz, seq // tm),
        in_specs=[
            pl.BlockSpec((None, tm, D_MODEL), lambda b, j: (b, j, 0)),
            pl.BlockSpec((None, 1, D_MODEL), layer_block),
            pl.BlockSpec((None, D_MODEL, A_WIDTH + B_WIDTH), layer_block),
        ],
        out_specs=[
            pl.BlockSpec((None, tm, A_WIDTH), lambda b, j: (b, j, 0)),
            pl.BlockSpec((None, tm, B_WIDTH), lambda b, j: (b, j, 0)),
        ],
        out_shape=[
            jax.ShapeDtypeStruct((bsz, seq, A_WIDTH), F32),
            jax.ShapeDtypeStruct((bsz, seq, B_WIDTH), F32),
        ],
        name="inproj",
        compiler_params=pltpu.CompilerParams(
            dimension_semantics=("arbitrary", "arbitrary"),
            vmem_limit_bytes=VMEM_LIMIT_PROJ),
    )(x, norm_g, w_in)


def _log_sigmoid(z):
    return jnp.minimum(z, 0.0) - jnp.log1p(jnp.exp(-jnp.abs(z)))


def _split3_bf16(x):
    hi = x.astype(BF16)
    r1 = x - hi.astype(F32)
    mid = r1.astype(BF16)
    lo = (r1 - mid.astype(F32)).astype(BF16)
    return hi, mid, lo


def _gla_direction(a_ref, w2_ref, b2_ref, st_ref, o_ref, reverse):
    tb = a_ref.shape[0]
    n_chunks = tb // GLA_CHUNK
    q = a_ref[:, A_Q:A_Q + GLA_KWIDTH]
    k = a_ref[:, A_K:A_K + GLA_KWIDTH]
    vb = a_ref[:, A_V:A_V + GLA_WIDTH].astype(BF16)
    lr = a_ref[:, A_LR:A_LR + LR_PAD].astype(BF16)

    z = _dot(lr, w2_ref[...]) + b2_ref[...]
    log_a = _log_sigmoid(z) * (1.0 / GLA_GATE_TAU)

    row = lax.broadcasted_iota(jnp.int32, (tb, tb), 0)
    col = lax.broadcasted_iota(jnp.int32, (tb, tb), 1)
    same_chunk = (row // GLA_CHUNK) == (col // GLA_CHUNK)
    tri = same_chunk & ((col >= row) if reverse else (col <= row))
    tri_b = jnp.where(tri, 1.0, 0.0).astype(BF16)
    hi, mid, lo = _split3_bf16(log_a)
    g_cum = _dot(tri_b, hi) + _dot(tri_b, mid) + _dot(tri_b, lo)

    def chunk_total_row(c):
        r = c * GLA_CHUNK + (0 if reverse else GLA_CHUNK - 1)
        return g_cum[r:r + 1, :]

    g_last = jnp.concatenate(
        [jnp.broadcast_to(chunk_total_row(c), (GLA_CHUNK, GLA_KWIDTH)) for c in range(n_chunks)], axis=0)

    q_dec = ((q * (GLA_DK ** -0.5)) * jnp.exp(g_cum)).astype(BF16)
    k_dec = (k * jnp.exp(-g_cum)).astype(BF16)
    k_state = (k * jnp.exp(g_last - g_cum)).astype(BF16)

    head_of_lane = lax.broadcasted_iota(jnp.int32, (tb, GLA_KWIDTH), 1) // GLA_DK
    zero_b = jnp.zeros((), BF16)

    o_intra = []
    for h in range(GLA_HEADS):
        q_h = jnp.where(head_of_lane == h, q_dec, zero_b)
        att = jnp.where(tri, _dot_nt(q_h, k_dec), 0.0).astype(BF16)
        o_intra.append(_dot(att, vb[:, h * GLA_DV:(h + 1) * GLA_DV]))

    head_of_lane_c = lax.broadcasted_iota(jnp.int32, (GLA_CHUNK, GLA_KWIDTH), 1) // GLA_DK
    head_of_lane_v = lax.broadcasted_iota(jnp.int32, (GLA_DV, GLA_KWIDTH), 1) // GLA_DK
    chunk_order = range(n_chunks - 1, -1, -1) if reverse else range(n_chunks)
    for c in chunk_order:
        r0 = c * GLA_CHUNK
        state = st_ref[...]
        q_c = q_dec[r0:r0 + GLA_CHUNK]
        q_stack = jnp.concatenate(
            [jnp.where(head_of_lane_c == h, q_c, zero_b) for h in range(GLA_HEADS)], axis=0)
        o_inter = _dot_nt(q_stack, state.astype(BF16))
        for h in range(GLA_HEADS):
            o_ref[r0:r0 + GLA_CHUNK, h * GLA_DV:(h + 1) * GLA_DV] = (
                o_intra[h][r0:r0 + GLA_CHUNK] + o_inter[h * GLA_CHUNK:(h + 1) * GLA_CHUNK])
        kv_all = _dot_tn(vb[r0:r0 + GLA_CHUNK], k_state[r0:r0 + GLA_CHUNK])
        kv_t = jnp.zeros((GLA_DV, GLA_KWIDTH), F32)
        for h in range(GLA_HEADS):
            kv_t = kv_t + jnp.where(head_of_lane_v == h, kv_all[h * GLA_DV:(h + 1) * GLA_DV], 0.0)
        st_ref[...] = state * jnp.exp(chunk_total_row(c)) + kv_t


def _gla_kernel(af_ref, ab_ref, w2_ref, b2_ref, of_ref, ob_ref, stf_ref, stb_ref):
    @pl.when(pl.program_id(1) == 0)
    def _():
        stf_ref[...] = jnp.zeros_like(stf_ref)
        stb_ref[...] = jnp.zeros_like(stb_ref)

    _gla_direction(af_ref, w2_ref.at[0], b2_ref.at[0], stf_ref, of_ref, reverse=False)
    _gla_direction(ab_ref, w2_ref.at[1], b2_ref.at[1], stb_ref, ob_ref, reverse=True)


def _gla(a, w2p, b2, layer):
    bsz, seq, _ = a.shape
    tb = TB_GLA
    n = seq // tb
    fwd = lambda b, i: (b, i, 0)
    bwd = lambda b, i: (b, n - 1 - i, 0)
    layer_block = lambda b, i: (layer, 0, 0, 0)
    return pl.pallas_call(
        _gla_kernel,
        grid=(bsz, n),
        in_specs=[
            pl.BlockSpec((None, tb, A_WIDTH), fwd),
            pl.BlockSpec((None, tb, A_WIDTH), bwd),
            pl.BlockSpec((None, 2, LR_PAD, GLA_KWIDTH), layer_block),
            pl.BlockSpec((None, 2, 1, GLA_KWIDTH), layer_block),
        ],
        out_specs=[
            pl.BlockSpec((None, tb, GLA_WIDTH), fwd),
            pl.BlockSpec((None, tb, GLA_WIDTH), bwd),
        ],
        out_shape=[jax.ShapeDtypeStruct((bsz, seq, GLA_WIDTH), F32)] * 2,
        scratch_shapes=[pltpu.VMEM((GLA_DV, GLA_KWIDTH), F32)] * 2,
        name="gla_scan",
        compiler_params=pltpu.CompilerParams(
            dimension_semantics=("arbitrary", "arbitrary"),
            vmem_limit_bytes=VMEM_LIMIT_GLA),
    )(a, a, w2p, b2)


def _halo_valid(tm, seq_len):
    s0 = pl.program_id(1) * tm
    return s0, s0 > 0, s0 + tm < seq_len


def _mix_kernel(of_ref, ob_ref, bm_ref, hp_ref, hn_ref, x_ref, gn_ref, lng_ref, lnb_ref,
                wst_ref, sb_ref, pw_ref, ps_ref, wo_ref, out_ref, *, seq_len):
    tm = x_ref.shape[0]
    s0, prev_ok, next_ok = _halo_valid(tm, seq_len)

    o = of_ref[...] + ob_ref[...]
    heads = []
    for h in range(GLA_HEADS):
        o_h = o[:, h * GLA_DV:(h + 1) * GLA_DV]
        heads.append(o_h * lax.rsqrt(jnp.mean(o_h * o_h, axis=-1, keepdims=True) + EPS))
    a_out = jnp.concatenate(heads, axis=-1) * gn_ref[...] * jax.nn.silu(bm_ref[:, B_G:B_G + GLA_WIDTH])

    gu = jax.nn.gelu(bm_ref[:, B_U:B_U + SGU_WIDTH])
    gv = jax.nn.gelu(bm_ref[:, B_VS:B_VS + SGU_WIDTH])
    xc = gv - jnp.mean(gv, axis=-1, keepdims=True)
    v_ln = xc * lax.rsqrt(jnp.mean(xc * xc, axis=-1, keepdims=True) + EPS) * lng_ref[...] + lnb_ref[...]
    v_b = v_ln.astype(BF16)
    head_of_lane = lax.broadcasted_iota(jnp.int32, (SGU_CHUNK, SGU_WIDTH), 1) // SGU_HEAD_DIM
    b_parts = []
    for n in range(tm // SGU_CHUNK):
        rows = slice(n * SGU_CHUNK, (n + 1) * SGU_CHUNK)
        all_heads = _dot(wst_ref[...], v_b[rows])
        mixed = jnp.zeros((SGU_CHUNK, SGU_WIDTH), F32)
        for h in range(SGU_HEADS):
            mixed = mixed + jnp.where(head_of_lane == h, all_heads[h * SGU_CHUNK:(h + 1) * SGU_CHUNK], 0.0)
        b_parts.append(gu[rows] * (mixed + sb_ref[...]))
    b_out = jnp.concatenate(b_parts, axis=0)

    xp = bm_ref[:, B_XP:B_XP + POOL_WIDTH]
    ext = jnp.concatenate(
        [jnp.where(prev_ok, hp_ref[...], 0.0), xp, jnp.where(next_ok, hn_ref[...], 0.0)], axis=0)
    rows_ext = tm + 2 * HALO
    back = lambda t, d: pltpu.roll(t, d, 0)
    ahead = lambda t, d: pltpu.roll(t, rows_ext - d, 0)
    s2 = ext + back(ext, 1)
    s4 = back(s2, 1) + ahead(s2, 1)
    s8 = back(s4, 2) + ahead(s4, 2)
    s16 = back(s8, 4) + ahead(s8, 4)
    core = lambda t: t[HALO:HALO + tm]
    group = lax.broadcasted_iota(jnp.int32, (tm, POOL_WIDTH), 1) // POOL_GROUP_DIM
    win = jnp.where(group == 0, core(s2), jnp.where(group == 1, core(s4),
                    jnp.where(group == 2, core(s8), core(s16))))
    half = jnp.where(group == 0, 1, jnp.where(group == 1, 2, jnp.where(group == 2, 4, 8)))
    t = s0 + lax.broadcasted_iota(jnp.int32, (tm, POOL_WIDTH), 0)
    cnt = jnp.minimum(t + half, seq_len) - jnp.maximum(t - half, 0)
    d = win / cnt.astype(F32) - xp
    c_out = _dot(d.astype(BF16), pw_ref[...]) * ps_ref[...]

    mix = jnp.concatenate([a_out, b_out, c_out], axis=-1).astype(BF16)
    out_ref[...] = x_ref[...] + _dot(mix, wo_ref[...])


def _mix(o_f, o_b, bm, x, p, layer):
    bsz, seq, _ = x.shape
    tm = TM_MIX
    hb = tm // HALO
    n_halo_blocks = seq // HALO
    tile = lambda b, j: (b, j, 0)
    layer_block = lambda b, j: (layer, 0, 0)
    xp_col_block = B_XP // POOL_WIDTH
    return pl.pallas_call(
        functools.partial(_mix_kernel, seq_len=seq),
        grid=(bsz, seq // tm),
        in_specs=[
            pl.BlockSpec((None, tm, GLA_WIDTH), tile),
            pl.BlockSpec((None, tm, GLA_WIDTH), tile),
            pl.BlockSpec((None, tm, B_WIDTH), tile),
            pl.BlockSpec((None, HALO, POOL_WIDTH),
                         lambda b, j: (b, jnp.maximum(j * hb - 1, 0), xp_col_block)),
            pl.BlockSpec((None, HALO, POOL_WIDTH),
                         lambda b, j: (b, jnp.minimum((j + 1) * hb, n_halo_blocks - 1), xp_col_block)),
            pl.BlockSpec((None, tm, D_MODEL), tile),
            pl.BlockSpec((None, 1, GLA_WIDTH), layer_block),
            pl.BlockSpec((None, 1, SGU_WIDTH), layer_block),
            pl.BlockSpec((None, 1, SGU_WIDTH), layer_block),
            pl.BlockSpec((None, SGU_HEADS * SGU_CHUNK, SGU_CHUNK), layer_block),
            pl.BlockSpec((None, SGU_CHUNK, SGU_WIDTH), layer_block),
            pl.BlockSpec((None, POOL_WIDTH, POOL_WIDTH), layer_block),
            pl.BlockSpec((None, 1, POOL_WIDTH), layer_block),
            pl.BlockSpec((None, D_MODEL, D_MODEL), layer_block),
        ],
        out_specs=pl.BlockSpec((None, tm, D_MODEL), tile),
        out_shape=jax.ShapeDtypeStruct((bsz, seq, D_MODEL), F32),
        name="mix_out",
        compiler_params=pltpu.CompilerParams(
            dimension_semantics=("arbitrary", "arbitrary"),
            vmem_limit_bytes=VMEM_LIMIT_MIX),
    )(o_f, o_b, bm, bm, bm, x, p["gla_norm_g"], p["sgu_ln_g"], p["sgu_ln_b"], p["sgu_w"],
      p["sgu_b"], p["pool_w"], p["pool_scale"], p["w_o"])


def _ffn_kernel(x_ref, hp_ref, hn_ref, g_ref, w1_ref, cw_ref, cb_ref, w2_ref, nf_ref, out_ref,
                hid_ref, *, seq_len, final):
    tm = x_ref.shape[0]
    _, prev_ok, next_ok = _halo_valid(tm, seq_len)
    x = x_ref[...]
    x_ext = jnp.concatenate(
        [jnp.where(prev_ok, hp_ref[...], 0.0), x, jnp.where(next_ok, hn_ref[...], 0.0)], axis=0)
    h_ext = _rmsnorm(x_ext, g_ref[...])
    hb_ext = h_ext.astype(BF16)
    hb = h_ext[HALO:HALO + tm].astype(BF16)
    rows_ext = tm + 2 * HALO
    for c0 in range(0, D_FF, FFN_COL_CHUNK):
        cols = slice(c0, c0 + FFN_COL_CHUNK)
        a = _dot(hb_ext, w1_ref[:, cols])
        up = _dot(hb, w1_ref[:, D_FF + c0:D_FF + c0 + FFN_COL_CHUNK])
        conv = (pltpu.roll(a, 1, 0) * cw_ref[0:1, cols] + a * cw_ref[1:2, cols]
                + pltpu.roll(a, rows_ext - 1, 0) * cw_ref[2:3, cols] + cb_ref[:, cols])
        hid_ref[:, cols] = (jax.nn.silu(conv[HALO:HALO + tm]) * up).astype(BF16)
    y = x + _dot(hid_ref[...], w2_ref[...])
    if final:
        y = _rmsnorm(y, nf_ref[...])
    out_ref[...] = y


def _ffn(x, p, layer, final):
    bsz, seq, _ = x.shape
    tm = TM_FFN
    hb = tm // HALO
    n_halo_blocks = seq // HALO
    tile = lambda b, j: (b, j, 0)
    layer_block = lambda b, j: (layer, 0, 0)
    return pl.pallas_call(
        functools.partial(_ffn_kernel, seq_len=seq, final=final),
        grid=(bsz, seq // tm),
        in_specs=[
            pl.BlockSpec((None, tm, D_MODEL), tile),
            pl.BlockSpec((None, HALO, D_MODEL), lambda b, j: (b, jnp.maximum(j * hb - 1, 0), 0)),
            pl.BlockSpec((None, HALO, D_MODEL),
                         lambda b, j: (b, jnp.minimum((j + 1) * hb, n_halo_blocks - 1), 0)),
            pl.BlockSpec((None, 1, D_MODEL), layer_block),
            pl.BlockSpec((None, D_MODEL, 2 * D_FF), layer_block),
            pl.BlockSpec((None, 3, D_FF), layer_block),
            pl.BlockSpec((None, 1, D_FF), layer_block),
            pl.BlockSpec((None, D_FF, D_MODEL), layer_block),
            pl.BlockSpec((1, D_MODEL), lambda b, j: (0, 0)),
        ],
        out_specs=pl.BlockSpec((None, tm, D_MODEL), tile),
        out_shape=jax.ShapeDtypeStruct((bsz, seq, D_MODEL), F32),
        scratch_shapes=[pltpu.VMEM((tm, D_FF), BF16)],
        name="conv_glu_ffn",
        compiler_params=pltpu.CompilerParams(
            dimension_semantics=("arbitrary", "arbitrary"),
            vmem_limit_bytes=VMEM_LIMIT_FFN),
    )(x, x, x, p["norm2_g"], p["w_ffn_in"], p["conv_w"], p["conv_b"], p["w_ffn_out"], p["norm_f"])


def _prepare_params(norm1_g, w_in, gla_gate_w2, gla_gate_b, gla_norm_g, sgu_ln_g, sgu_ln_b,
                    sgu_w, sgu_b, pool_w, pool_scale, w_o, norm2_g, w_ffn_in, conv_w, conv_b,
                    w_ffn_out, norm_f):
    depth = w_in.shape[0]
    row = lambda t: t.reshape(depth, 1, t.shape[-1])
    c_qkv = 2 * GLA_KWIDTH + GLA_WIDTH
    c_g = c_qkv + GLA_WIDTH
    c_lr = c_g + 2 * GLA_GATE_RANK
    w_in_p = jnp.concatenate([
        w_in[:, :, :c_qkv], w_in[:, :, c_g:c_lr],
        jnp.zeros((depth, D_MODEL, LR_PAD - 2 * GLA_GATE_RANK), w_in.dtype),
        w_in[:, :, c_qkv:c_g], w_in[:, :, c_lr:]], axis=-1).astype(BF16)
    w2p = jnp.zeros((depth, 2, LR_PAD, GLA_KWIDTH), F32)
    for d in range(2):
        w2p = w2p.at[:, d, d * GLA_GATE_RANK:(d + 1) * GLA_GATE_RANK].set(gla_gate_w2[:, d])
    pool_bd = jnp.einsum("lgcd,gh->lgchd", pool_w, jnp.eye(len(POOL_WINDOWS), dtype=pool_w.dtype))
    return {
        "norm1_g": row(norm1_g),
        "w_in": w_in_p,
        "gate_w2": w2p.astype(BF16),
        "gate_b": gla_gate_b.reshape(depth, 2, 1, GLA_KWIDTH),
        "gla_norm_g": row(gla_norm_g),
        "sgu_ln_g": row(sgu_ln_g),
        "sgu_ln_b": row(sgu_ln_b),
        "sgu_w": sgu_w.reshape(depth, SGU_HEADS * SGU_CHUNK, SGU_CHUNK).astype(BF16),
        "sgu_b": jnp.repeat(jnp.swapaxes(sgu_b, 1, 2), SGU_HEAD_DIM, axis=2),
        "pool_w": pool_bd.reshape(depth, POOL_WIDTH, POOL_WIDTH).astype(BF16),
        "pool_scale": row(pool_scale),
        "w_o": w_o.astype(BF16),
        "norm2_g": row(norm2_g),
        "w_ffn_in": w_ffn_in.astype(BF16),
        "conv_w": conv_w,
        "conv_b": row(conv_b),
        "w_ffn_out": w_ffn_out.astype(BF16),
        "norm_f": norm_f.reshape(1, D_MODEL),
    }


def _trunk(x, p):
    depth = p["w_in"].shape[0]
    for layer in range(depth):
        a, bm = _inproj(x, p["norm1_g"], p["w_in"], layer)
        o_f, o_b = _gla(a, p["gate_w2"], p["gate_b"], layer)
        x = _mix(o_f, o_b, bm, x, p, layer)
        x = _ffn(x, p, layer, final=(layer == depth - 1))
    return x


def kernel(x_prompt, x_sample, norm1_g, w_in, gla_gate_w2, gla_gate_b, gla_norm_g, sgu_ln_g, sgu_ln_b, sgu_w, sgu_b, pool_w, pool_scale, w_o, norm2_g, w_ffn_in, conv_w, conv_b, w_ffn_out, norm_f):
    p = _prepare_params(norm1_g, w_in, gla_gate_w2, gla_gate_b, gla_norm_g, sgu_ln_g, sgu_ln_b,
                        sgu_w, sgu_b, pool_w, pool_scale, w_o, norm2_g, w_ffn_in, conv_w, conv_b,
                        w_ffn_out, norm_f)
    return (_trunk(x_prompt, p), _trunk(x_sample, p))
```

```python
import functools

import jax
import jax.numpy as jnp
from jax import lax
from jax.experimental import pallas as pl
from jax.experimental.pallas import tpu as pltpu

F32 = jnp.float32
BF16 = jnp.bfloat16

D_MODEL = 1024
DEPTH = 4
GLA_HEADS = 4
GLA_WIDTH = 512
GLA_DV = 128
GLA_DK = 64
GLA_KWIDTH = 256
GLA_GATE_RANK = 16
GLA_GATE_TAU = 16.0
GLA_CHUNK = 64
SGU_HEADS = 4
SGU_WIDTH = 256
SGU_HEAD_DIM = 64
SGU_CHUNK = 128
POOL_WINDOWS = (2, 4, 8, 16)
POOL_WIDTH = 256
POOL_GROUP_DIM = 64
POOL_HALO = max(POOL_WINDOWS) // 2
D_FF = 2816
EPS = 1e-6

V7X_LANES = 128
V7X_SUBLANES = 8
V7X_VMEM_BYTES = 64 * 1024 * 1024

LR_PAD = V7X_LANES
A_WIDTH = 2 * GLA_KWIDTH + GLA_WIDTH + LR_PAD
B_WIDTH = GLA_WIDTH + 2 * SGU_WIDTH
A_Q, A_K, A_V, A_LR = 0, GLA_KWIDTH, 2 * GLA_KWIDTH, 2 * GLA_KWIDTH + GLA_WIDTH
B_G, B_U, B_VS = 0, GLA_WIDTH, GLA_WIDTH + SGU_WIDTH
IN_WIDTH_PADDED = A_WIDTH + B_WIDTH + POOL_WIDTH

HALO = V7X_SUBLANES

TM_PROJ = 512
TB_GLA = 256
TM_MIX = 512
TM_FFN = 512
FFN_COL_CHUNK = 256

VMEM_LIMIT_PROJ = 40 * 1024 * 1024
VMEM_LIMIT_GLA = 32 * 1024 * 1024
VMEM_LIMIT_MIX = 48 * 1024 * 1024
VMEM_LIMIT_FFN = 56 * 1024 * 1024


def _rmsnorm(x, g):
    return x * lax.rsqrt(jnp.mean(x * x, axis=-1, keepdims=True) + EPS) * g


def _dot(a, b):
    return jnp.dot(a, b, preferred_element_type=F32)


def _dot_nt(a, b):
    return lax.dot_general(a, b, (((1,), (1,)), ((), ())), preferred_element_type=F32)


def _dot_tn(a, b):
    return lax.dot_general(a, b, (((0,), (0,)), ((), ())), preferred_element_type=F32)


def _inproj_kernel(x_ref, g_ref, w_ref, a_ref, b_ref, xp_ref):
    hb = _rmsnorm(x_ref[...], g_ref[...]).astype(BF16)
    a_ref[...] = _dot(hb, w_ref[:, :A_WIDTH]).astype(BF16)
    b_ref[...] = _dot(hb, w_ref[:, A_WIDTH:A_WIDTH + B_WIDTH]).astype(BF16)
    xp_ref[...] = _dot(hb, w_ref[:, A_WIDTH + B_WIDTH:])


def _inproj(x, norm_g, w_in, layer):
    bsz, seq, _ = x.shape
    tm = TM_PROJ
    layer_block = lambda b, j: (layer, 0, 0)
    return pl.pallas_call(
        _inproj_kernel,
        grid=(bsz, seq // tm),
        in_specs=[
            pl.BlockSpec((None, tm, D_MODEL), lambda b, j: (b, j, 0)),
            pl.BlockSpec((None, 1, D_MODEL), layer_block),
            pl.BlockSpec((None, D_MODEL, IN_WIDTH_PADDED), layer_block),
        ],
        out_specs=[
            pl.BlockSpec((None, tm, A_WIDTH), lambda b, j: (b, j, 0)),
            pl.BlockSpec((None, tm, B_WIDTH), lambda b, j: (b, j, 0)),
            pl.BlockSpec((None, tm, POOL_WIDTH), lambda b, j: (b, j, 0)),
        ],
        out_shape=[
            jax.ShapeDtypeStruct((bsz, seq, A_WIDTH), BF16),
            jax.ShapeDtypeStruct((bsz, seq, B_WIDTH), BF16),
            jax.ShapeDtypeStruct((bsz, seq, POOL_WIDTH), F32),
        ],
        name="inproj",
        compiler_params=pltpu.CompilerParams(
            dimension_semantics=("arbitrary", "arbitrary"),
            vmem_limit_bytes=VMEM_LIMIT_PROJ),
    )(x, norm_g, w_in)


def _log_sigmoid(z):
    return jnp.minimum(z, 0.0) - jnp.log1p(jnp.exp(-jnp.abs(z)))


def _split3_bf16(x):
    hi = x.astype(BF16)
    r1 = x - hi.astype(F32)
    mid = r1.astype(BF16)
    lo = (r1 - mid.astype(F32)).astype(BF16)
    return hi, mid, lo


def _gla_prepare(a_ref, w2_ref, b2_ref, reverse):
    tb = a_ref.shape[0]
    n_chunks = tb // GLA_CHUNK
    q = a_ref[:, A_Q:A_Q + GLA_KWIDTH].astype(F32)
    k = a_ref[:, A_K:A_K + GLA_KWIDTH].astype(F32)
    vb = a_ref[:, A_V:A_V + GLA_WIDTH]
    lr = a_ref[:, A_LR:A_LR + LR_PAD]

    z = _dot(lr, w2_ref[...]) + b2_ref[...]
    log_a = _log_sigmoid(z) * (1.0 / GLA_GATE_TAU)

    row = lax.broadcasted_iota(jnp.int32, (tb, tb), 0)
    col = lax.broadcasted_iota(jnp.int32, (tb, tb), 1)
    same_chunk = (row // GLA_CHUNK) == (col // GLA_CHUNK)
    tri = same_chunk & ((col >= row) if reverse else (col <= row))
    tri_b = jnp.where(tri, 1.0, 0.0).astype(BF16)
    hi, mid, lo = _split3_bf16(log_a)
    g_cum = _dot(tri_b, hi) + _dot(tri_b, mid) + _dot(tri_b, lo)

    total_rows = [c * GLA_CHUNK + (0 if reverse else GLA_CHUNK - 1) for c in range(n_chunks)]
    totals = [g_cum[r:r + 1, :] for r in total_rows]
    g_last = jnp.concatenate(
        [jnp.broadcast_to(t, (GLA_CHUNK, GLA_KWIDTH)) for t in totals], axis=0)
    dec_cols = [jnp.transpose(jnp.broadcast_to(jnp.exp(t), (GLA_DV, GLA_KWIDTH))) for t in totals]

    return {
        "q_dec": ((q * (GLA_DK ** -0.5)) * jnp.exp(g_cum)).astype(BF16),
        "k_dec": (k * jnp.exp(-g_cum)).astype(BF16),
        "k_state": (k * jnp.exp(g_last - g_cum)).astype(BF16),
        "v": vb,
        "dec_cols": dec_cols,
    }


def _gla_chunk_terms(pre, c, reverse):
    rows = slice(c * GLA_CHUNK, (c + 1) * GLA_CHUNK)
    q_c, k_c, ks_c, v_c = pre["q_dec"][rows], pre["k_dec"][rows], pre["k_state"][rows], pre["v"][rows]
    zero_b = jnp.zeros((), BF16)
    k_head = lax.broadcasted_iota(jnp.int32, (GLA_CHUNK, GLA_KWIDTH), 1) // GLA_DK
    v_head = lax.broadcasted_iota(jnp.int32, (GLA_CHUNK, GLA_WIDTH), 1) // GLA_DV
    k_bd = jnp.concatenate([jnp.where(k_head == h, k_c, zero_b) for h in range(GLA_HEADS)], axis=0)
    v_bd = jnp.concatenate([jnp.where(v_head == h, v_c, zero_b) for h in range(GLA_HEADS)], axis=0)
    i = lax.broadcasted_iota(jnp.int32, (GLA_CHUNK, GLA_HEADS * GLA_CHUNK), 0)
    j = lax.broadcasted_iota(jnp.int32, (GLA_CHUNK, GLA_HEADS * GLA_CHUNK), 1) % GLA_CHUNK
    keep = (j >= i) if reverse else (j <= i)
    att = jnp.where(keep, _dot_nt(q_c, k_bd), 0.0).astype(BF16)
    kv = []
    for pair in range(GLA_HEADS // 2):
        both = _dot_tn(ks_c[:, pair * 2 * GLA_DK:(pair + 1) * 2 * GLA_DK],
                       v_c[:, pair * 2 * GLA_DV:(pair + 1) * 2 * GLA_DV])
        kv.append(both[:GLA_DK, :GLA_DV])
        kv.append(both[GLA_DK:, GLA_DV:])
    return {"q": q_c, "att": att, "v_bd": v_bd, "kv": kv, "dec": pre["dec_cols"][c]}


def _gla_chunk_step(terms, state, o_ref, c):
    zero_blk = jnp.zeros((GLA_DK, GLA_DV), BF16)
    s_bd = jnp.concatenate(
        [jnp.concatenate([state[h].astype(BF16) if g == h else zero_blk for g in range(GLA_HEADS)], axis=1)
         for h in range(GLA_HEADS)], axis=0)
    lhs = jnp.concatenate([terms["att"], terms["q"]], axis=1)
    rhs = jnp.concatenate([terms["v_bd"], s_bd], axis=0)
    o_ref[c * GLA_CHUNK:(c + 1) * GLA_CHUNK, :] = _dot(lhs, rhs).astype(o_ref.dtype)
    dec = terms["dec"]
    return [dec[h * GLA_DK:(h + 1) * GLA_DK] * state[h] + terms["kv"][h] for h in range(GLA_HEADS)]


def _gla_kernel(af_ref, ab_ref, w2_ref, b2_ref, of_ref, ob_ref, stf_ref, stb_ref):
    @pl.when(pl.program_id(1) == 0)
    def _():
        stf_ref[...] = jnp.zeros_like(stf_ref)
        stb_ref[...] = jnp.zeros_like(stb_ref)

    n_chunks = af_ref.shape[0] // GLA_CHUNK
    pre_f = _gla_prepare(af_ref, w2_ref.at[0], b2_ref.at[0], reverse=False)
    pre_b = _gla_prepare(ab_ref, w2_ref.at[1], b2_ref.at[1], reverse=True)
    terms_f = [_gla_chunk_terms(pre_f, c, reverse=False) for c in range(n_chunks)]
    terms_b = [_gla_chunk_terms(pre_b, c, reverse=True) for c in range(n_chunks)]
    state_f = [stf_ref[h] for h in range(GLA_HEADS)]
    state_b = [stb_ref[h] for h in range(GLA_HEADS)]
    for step in range(n_chunks):
        cf, cb = step, n_chunks - 1 - step
        state_f = _gla_chunk_step(terms_f[cf], state_f, of_ref, cf)
        state_b = _gla_chunk_step(terms_b[cb], state_b, ob_ref, cb)
    for h in range(GLA_HEADS):
        stf_ref[h] = state_f[h]
        stb_ref[h] = state_b[h]


def _gla(a, w2p, b2, layer):
    bsz, seq, _ = a.shape
    tb = TB_GLA
    n = seq // tb
    fwd = lambda b, i: (b, i, 0)
    bwd = lambda b, i: (b, n - 1 - i, 0)
    layer_block = lambda b, i: (layer, 0, 0, 0)
    return pl.pallas_call(
        _gla_kernel,
        grid=(bsz, n),
        in_specs=[
            pl.BlockSpec((None, tb, A_WIDTH), fwd),
            pl.BlockSpec((None, tb, A_WIDTH), bwd),
            pl.BlockSpec((None, 2, LR_PAD, GLA_KWIDTH), layer_block),
            pl.BlockSpec((None, 2, 1, GLA_KWIDTH), layer_block),
        ],
        out_specs=[
            pl.BlockSpec((None, tb, GLA_WIDTH), fwd),
            pl.BlockSpec((None, tb, GLA_WIDTH), bwd),
        ],
        out_shape=[jax.ShapeDtypeStruct((bsz, seq, GLA_WIDTH), BF16)] * 2,
        scratch_shapes=[pltpu.VMEM((GLA_HEADS, GLA_DK, GLA_DV), F32)] * 2,
        name="gla_scan",
        compiler_params=pltpu.CompilerParams(
            dimension_semantics=("arbitrary", "arbitrary"),
            vmem_limit_bytes=VMEM_LIMIT_GLA),
    )(a, a, w2p, b2)


def _halo_valid(tm, seq_len):
    s0 = pl.program_id(1) * tm
    return s0, s0 > 0, s0 + tm < seq_len


def _mix_kernel(of_ref, ob_ref, bm_ref, xp_ref, hp_ref, hn_ref, x_ref, gn_ref, lng_ref, lnb_ref,
                wst_ref, sb_ref, pw_ref, ps_ref, wo_ref, out_ref, *, seq_len):
    tm = x_ref.shape[0]
    s0, prev_ok, next_ok = _halo_valid(tm, seq_len)

    o = of_ref[...].astype(F32) + ob_ref[...].astype(F32)
    heads = []
    for h in range(GLA_HEADS):
        o_h = o[:, h * GLA_DV:(h + 1) * GLA_DV]
        heads.append(o_h * lax.rsqrt(jnp.mean(o_h * o_h, axis=-1, keepdims=True) + EPS))
    gate = bm_ref[:, B_G:B_G + GLA_WIDTH].astype(F32)
    a_out = jnp.concatenate(heads, axis=-1) * gn_ref[...] * jax.nn.silu(gate)

    gu = jax.nn.gelu(bm_ref[:, B_U:B_U + SGU_WIDTH].astype(F32))
    gv = jax.nn.gelu(bm_ref[:, B_VS:B_VS + SGU_WIDTH].astype(F32))
    xc = gv - jnp.mean(gv, axis=-1, keepdims=True)
    v_ln = xc * lax.rsqrt(jnp.mean(xc * xc, axis=-1, keepdims=True) + EPS) * lng_ref[...] + lnb_ref[...]
    v_b = v_ln.astype(BF16)
    head_of_lane = lax.broadcasted_iota(jnp.int32, (SGU_CHUNK, SGU_WIDTH), 1) // SGU_HEAD_DIM
    b_parts = []
    for n in range(tm // SGU_CHUNK):
        rows = slice(n * SGU_CHUNK, (n + 1) * SGU_CHUNK)
        all_heads = _dot(wst_ref[...], v_b[rows])
        mixed = jnp.zeros((SGU_CHUNK, SGU_WIDTH), F32)
        for h in range(SGU_HEADS):
            mixed = mixed + jnp.where(head_of_lane == h, all_heads[h * SGU_CHUNK:(h + 1) * SGU_CHUNK], 0.0)
        b_parts.append(gu[rows] * (mixed + sb_ref[...]))
    b_out = jnp.concatenate(b_parts, axis=0)

    xp = xp_ref[...]
    ext = jnp.concatenate(
        [jnp.where(prev_ok, hp_ref[...], 0.0), xp, jnp.where(next_ok, hn_ref[...], 0.0)], axis=0)
    rows_ext = tm + 2 * HALO
    back = lambda t, d: pltpu.roll(t, d, 0)
    ahead = lambda t, d: pltpu.roll(t, rows_ext - d, 0)
    s2 = ext + back(ext, 1)
    s4 = back(s2, 1) + ahead(s2, 1)
    s8 = back(s4, 2) + ahead(s4, 2)
    s16 = back(s8, 4) + ahead(s8, 4)
    core = lambda t: t[HALO:HALO + tm]
    group = lax.broadcasted_iota(jnp.int32, (tm, POOL_WIDTH), 1) // POOL_GROUP_DIM
    win = jnp.where(group == 0, core(s2), jnp.where(group == 1, core(s4),
                    jnp.where(group == 2, core(s8), core(s16))))
    half = jnp.where(group == 0, 1, jnp.where(group == 1, 2, jnp.where(group == 2, 4, 8)))
    t = s0 + lax.broadcasted_iota(jnp.int32, (tm, POOL_WIDTH), 0)
    cnt = jnp.minimum(t + half, seq_len) - jnp.maximum(t - half, 0)
    d = win / cnt.astype(F32) - xp
    c_out = _dot(d.astype(BF16), pw_ref[...]) * ps_ref[...]

    mix = jnp.concatenate([a_out, b_out, c_out], axis=-1).astype(BF16)
    out_ref[...] = x_ref[...] + _dot(mix, wo_ref[...])


def _mix(o_f, o_b, bm, xp, x, p, layer):
    bsz, seq, _ = x.shape
    tm = TM_MIX
    hb = tm // HALO
    n_halo_blocks = seq // HALO
    tile = lambda b, j: (b, j, 0)
    layer_block = lambda b, j: (layer, 0, 0)
    return pl.pallas_call(
        functools.partial(_mix_kernel, seq_len=seq),
        grid=(bsz, seq // tm),
        in_specs=[
            pl.BlockSpec((None, tm, GLA_WIDTH), tile),
            pl.BlockSpec((None, tm, GLA_WIDTH), tile),
            pl.BlockSpec((None, tm, B_WIDTH), tile),
            pl.BlockSpec((None, tm, POOL_WIDTH), tile),
            pl.BlockSpec((None, HALO, POOL_WIDTH), lambda b, j: (b, jnp.maximum(j * hb - 1, 0), 0)),
            pl.BlockSpec((None, HALO, POOL_WIDTH),
                         lambda b, j: (b, jnp.minimum((j + 1) * hb, n_halo_blocks - 1), 0)),
            pl.BlockSpec((None, tm, D_MODEL), tile),
            pl.BlockSpec((None, 1, GLA_WIDTH), layer_block),
            pl.BlockSpec((None, 1, SGU_WIDTH), layer_block),
            pl.BlockSpec((None, 1, SGU_WIDTH), layer_block),
            pl.BlockSpec((None, SGU_HEADS * SGU_CHUNK, SGU_CHUNK), layer_block),
            pl.BlockSpec((None, SGU_CHUNK, SGU_WIDTH), layer_block),
            pl.BlockSpec((None, POOL_WIDTH, POOL_WIDTH), layer_block),
            pl.BlockSpec((None, 1, POOL_WIDTH), layer_block),
            pl.BlockSpec((None, D_MODEL, D_MODEL), layer_block),
        ],
        out_specs=pl.BlockSpec((None, tm, D_MODEL), tile),
        out_shape=jax.ShapeDtypeStruct((bsz, seq, D_MODEL), F32),
        name="mix_out",
        compiler_params=pltpu.CompilerParams(
            dimension_semantics=("arbitrary", "arbitrary"),
            vmem_limit_bytes=VMEM_LIMIT_MIX),
    )(o_f, o_b, bm, xp, xp, xp, x, p["gla_norm_g"], p["sgu_ln_g"], p["sgu_ln_b"], p["sgu_w"],
      p["sgu_b"], p["pool_w"], p["pool_scale"], p["w_o"])


def _ffn_kernel(x_ref, hp_ref, hn_ref, g_ref, w1_ref, cw_ref, cb_ref, w2_ref, nf_ref, out_ref,
                hid_ref, *, seq_len, final):
    tm = x_ref.shape[0]
    _, prev_ok, next_ok = _halo_valid(tm, seq_len)
    x = x_ref[...]
    x_ext = jnp.concatenate(
        [jnp.where(prev_ok, hp_ref[...], 0.0), x, jnp.where(next_ok, hn_ref[...], 0.0)], axis=0)
    h_ext = _rmsnorm(x_ext, g_ref[...])
    hb_ext = h_ext.astype(BF16)
    hb = h_ext[HALO:HALO + tm].astype(BF16)
    rows_ext = tm + 2 * HALO
    for c0 in range(0, D_FF, FFN_COL_CHUNK):
        cols = slice(c0, c0 + FFN_COL_CHUNK)
        a = _dot(hb_ext, w1_ref[:, cols])
        up = _dot(hb, w1_ref[:, D_FF + c0:D_FF + c0 + FFN_COL_CHUNK])
        conv = (pltpu.roll(a, 1, 0) * cw_ref[0:1, cols] + a * cw_ref[1:2, cols]
                + pltpu.roll(a, rows_ext - 1, 0) * cw_ref[2:3, cols] + cb_ref[:, cols])
        hid_ref[:, cols] = (jax.nn.silu(conv[HALO:HALO + tm]) * up).astype(BF16)
    y = x + _dot(hid_ref[...], w2_ref[...])
    if final:
        y = _rmsnorm(y, nf_ref[...])
    out_ref[...] = y


def _ffn(x, p, layer, final):
    bsz, seq, _ = x.shape
    tm = TM_FFN
    hb = tm // HALO
    n_halo_blocks = seq // HALO
    tile = lambda b, j: (b, j, 0)
    layer_block = lambda b, j: (layer, 0, 0)
    return pl.pallas_call(
        functools.partial(_ffn_kernel, seq_len=seq, final=final),
        grid=(bsz, seq // tm),
        in_specs=[
            pl.BlockSpec((None, tm, D_MODEL), tile),
            pl.BlockSpec((None, HALO, D_MODEL), lambda b, j: (b, jnp.maximum(j * hb - 1, 0), 0)),
            pl.BlockSpec((None, HALO, D_MODEL),
                         lambda b, j: (b, jnp.minimum((j + 1) * hb, n_halo_blocks - 1), 0)),
            pl.BlockSpec((None, 1, D_MODEL), layer_block),
            pl.BlockSpec((None, D_MODEL, 2 * D_FF), layer_block),
            pl.BlockSpec((None, 3, D_FF), layer_block),
            pl.BlockSpec((None, 1, D_FF), layer_block),
            pl.BlockSpec((None, D_FF, D_MODEL), layer_block),
            pl.BlockSpec((1, D_MODEL), lambda b, j: (0, 0)),
        ],
        out_specs=pl.BlockSpec((None, tm, D_MODEL), tile),
        out_shape=jax.ShapeDtypeStruct((bsz, seq, D_MODEL), F32),
        scratch_shapes=[pltpu.VMEM((tm, D_FF), BF16)],
        name="conv_glu_ffn",
        compiler_params=pltpu.CompilerParams(
            dimension_semantics=("arbitrary", "arbitrary"),
            vmem_limit_bytes=VMEM_LIMIT_FFN),
    )(x, x, x, p["norm2_g"], p["w_ffn_in"], p["conv_w"], p["conv_b"], p["w_ffn_out"], p["norm_f"])


def _prepare_params(norm1_g, w_in, gla_gate_w2, gla_gate_b, gla_norm_g, sgu_ln_g, sgu_ln_b,
                    sgu_w, sgu_b, pool_w, pool_scale, w_o, norm2_g, w_ffn_in, conv_w, conv_b,
                    w_ffn_out, norm_f):
    depth = w_in.shape[0]
    row = lambda t: t.reshape(depth, 1, t.shape[-1])
    c_qkv = 2 * GLA_KWIDTH + GLA_WIDTH
    c_g = c_qkv + GLA_WIDTH
    c_lr = c_g + 2 * GLA_GATE_RANK
    w_in_p = jnp.concatenate([
        w_in[:, :, :c_qkv], w_in[:, :, c_g:c_lr],
        jnp.zeros((depth, D_MODEL, LR_PAD - 2 * GLA_GATE_RANK), w_in.dtype),
        w_in[:, :, c_qkv:c_g], w_in[:, :, c_lr:]], axis=-1).astype(BF16)
    w2p = jnp.zeros((depth, 2, LR_PAD, GLA_KWIDTH), F32)
    for d in range(2):
        w2p = w2p.at[:, d, d * GLA_GATE_RANK:(d + 1) * GLA_GATE_RANK].set(gla_gate_w2[:, d])
    pool_bd = jnp.einsum("lgcd,gh->lgchd", pool_w, jnp.eye(len(POOL_WINDOWS), dtype=pool_w.dtype))
    return {
        "norm1_g": row(norm1_g),
        "w_in": w_in_p,
        "gate_w2": w2p.astype(BF16),
        "gate_b": gla_gate_b.reshape(depth, 2, 1, GLA_KWIDTH),
        "gla_norm_g": row(gla_norm_g),
        "sgu_ln_g": row(sgu_ln_g),
        "sgu_ln_b": row(sgu_ln_b),
        "sgu_w": sgu_w.reshape(depth, SGU_HEADS * SGU_CHUNK, SGU_CHUNK).astype(BF16),
        "sgu_b": jnp.repeat(jnp.swapaxes(sgu_b, 1, 2), SGU_HEAD_DIM, axis=2),
        "pool_w": pool_bd.reshape(depth, POOL_WIDTH, POOL_WIDTH).astype(BF16),
        "pool_scale": row(pool_scale),
        "w_o": w_o.astype(BF16),
        "norm2_g": row(norm2_g),
        "w_ffn_in": w_ffn_in.astype(BF16),
        "conv_w": conv_w,
        "conv_b": row(conv_b),
        "w_ffn_out": w_ffn_out.astype(BF16),
        "norm_f": norm_f.reshape(1, D_MODEL),
    }


def _trunk(x, p):
    depth = p["w_in"].shape[0]
    for layer in range(depth):
        a, bm, xp = _inproj(x, p["norm1_g"], p["w_in"], layer)
        o_f, o_b = _gla(a, p["gate_w2"], p["gate_b"], layer)
        x = _mix(o_f, o_b, bm, xp, x, p, layer)
        x = _ffn(x, p, layer, final=(layer == depth - 1))
    return x


def kernel(x_prompt, x_sample, norm1_g, w_in, gla_gate_w2, gla_gate_b, gla_norm_g, sgu_ln_g, sgu_ln_b, sgu_w, sgu_b, pool_w, pool_scale, w_o, norm2_g, w_ffn_in, conv_w, conv_b, w_ffn_out, norm_f):
    p = _prepare_params(norm1_g, w_in, gla_gate_w2, gla_gate_b, gla_norm_g, sgu_ln_g, sgu_ln_b,
                        sgu_w, sgu_b, pool_w, pool_scale, w_o, norm2_g, w_ffn_in, conv_w, conv_b,
                        w_ffn_out, norm_f)
    return (_trunk(x_prompt, p), _trunk(x_sample, p))
```

```python
import functools

import jax
import jax.numpy as jnp
from jax import lax
from jax.experimental import pallas as pl
from jax.experimental.pallas import tpu as pltpu

F32 = jnp.float32
BF16 = jnp.bfloat16

D_MODEL = 1024
DEPTH = 4
GLA_HEADS = 4
GLA_WIDTH = 512
GLA_DV = 128
GLA_DK = 64
GLA_KWIDTH = 256
GLA_GATE_RANK = 16
GLA_GATE_TAU = 16.0
GLA_CHUNK = 64
SGU_HEADS = 4
SGU_WIDTH = 256
SGU_HEAD_DIM = 64
SGU_CHUNK = 128
POOL_WINDOWS = (2, 4, 8, 16)
POOL_WIDTH = 256
POOL_GROUP_DIM = 64
POOL_HALO = max(POOL_WINDOWS) // 2
D_FF = 2816
EPS = 1e-6

V7X_LANES = 128
V7X_SUBLANES = 8
V7X_VMEM_BYTES = 64 * 1024 * 1024

LR_PAD = V7X_LANES
A_WIDTH = 2 * GLA_KWIDTH + GLA_WIDTH + LR_PAD
B_WIDTH = GLA_WIDTH + 2 * SGU_WIDTH
A_Q, A_K, A_V, A_LR = 0, GLA_KWIDTH, 2 * GLA_KWIDTH, 2 * GLA_KWIDTH + GLA_WIDTH
B_G, B_U, B_VS = 0, GLA_WIDTH, GLA_WIDTH + SGU_WIDTH
IN_WIDTH_PADDED = A_WIDTH + B_WIDTH + POOL_WIDTH

HALO = V7X_SUBLANES

TM_PROJ = 512
TB_GLA = 256
TM_MIX = 512
TM_FFN = 512
FFN_COL_CHUNK = 256

VMEM_LIMIT_PROJ = 40 * 1024 * 1024
VMEM_LIMIT_GLA = 32 * 1024 * 1024
VMEM_LIMIT_MIX = 48 * 1024 * 1024
VMEM_LIMIT_FFN = 56 * 1024 * 1024


def _rmsnorm(x, g):
    return x * lax.rsqrt(jnp.mean(x * x, axis=-1, keepdims=True) + EPS) * g


def _dot(a, b):
    return jnp.dot(a, b, preferred_element_type=F32)


def _dot_nt(a, b):
    return lax.dot_general(a, b, (((1,), (1,)), ((), ())), preferred_element_type=F32)


def _dot_tn(a, b):
    return lax.dot_general(a, b, (((0,), (0,)), ((), ())), preferred_element_type=F32)


def _inproj_kernel(x_ref, g_ref, w_ref, a_ref, b_ref, xp_ref):
    hb = _rmsnorm(x_ref[...], g_ref[...]).astype(BF16)
    a_ref[...] = _dot(hb, w_ref[:, :A_WIDTH]).astype(BF16)
    b_ref[...] = _dot(hb, w_ref[:, A_WIDTH:A_WIDTH + B_WIDTH]).astype(BF16)
    xp_ref[...] = _dot(hb, w_ref[:, A_WIDTH + B_WIDTH:])


def _inproj(x, norm_g, w_in, layer):
    bsz, seq, _ = x.shape
    tm = TM_PROJ
    layer_block = lambda b, j: (layer, 0, 0)
    return pl.pallas_call(
        _inproj_kernel,
        grid=(bsz, seq // tm),
        in_specs=[
            pl.BlockSpec((None, tm, D_MODEL), lambda b, j: (b, j, 0)),
            pl.BlockSpec((None, 1, D_MODEL), layer_block),
            pl.BlockSpec((None, D_MODEL, IN_WIDTH_PADDED), layer_block),
        ],
        out_specs=[
            pl.BlockSpec((None, tm, A_WIDTH), lambda b, j: (b, j, 0)),
            pl.BlockSpec((None, tm, B_WIDTH), lambda b, j: (b, j, 0)),
            pl.BlockSpec((None, tm, POOL_WIDTH), lambda b, j: (b, j, 0)),
        ],
        out_shape=[
            jax.ShapeDtypeStruct((bsz, seq, A_WIDTH), BF16),
            jax.ShapeDtypeStruct((bsz, seq, B_WIDTH), BF16),
            jax.ShapeDtypeStruct((bsz, seq, POOL_WIDTH), F32),
        ],
        name="inproj",
        compiler_params=pltpu.CompilerParams(
            dimension_semantics=("arbitrary", "arbitrary"),
            vmem_limit_bytes=VMEM_LIMIT_PROJ),
    )(x, norm_g, w_in)


def _log_sigmoid(z):
    return jnp.minimum(z, 0.0) - jnp.log(1.0 + jnp.exp(-jnp.abs(z)))


def _split3_bf16(x):
    hi = x.astype(BF16)
    r1 = x - hi.astype(F32)
    mid = r1.astype(BF16)
    lo = (r1 - mid.astype(F32)).astype(BF16)
    return hi, mid, lo


def _gla_log_decay(a_ref, w2_ref, b2_ref):
    z = _dot(a_ref[:, A_LR:A_LR + LR_PAD], w2_ref[...]) + b2_ref[...]
    return _log_sigmoid(z) * (1.0 / GLA_GATE_TAU)


def _gla_cumsum(log_a, reverse):
    tb = log_a.shape[0]
    row = lax.broadcasted_iota(jnp.int32, (tb, tb), 0)
    col = lax.broadcasted_iota(jnp.int32, (tb, tb), 1)
    same_chunk = (row // GLA_CHUNK) == (col // GLA_CHUNK)
    tri = same_chunk & ((col >= row) if reverse else (col <= row))
    tri_b = jnp.where(tri, 1.0, 0.0).astype(BF16)
    hi, mid, lo = _split3_bf16(log_a)
    return _dot(tri_b, hi) + _dot(tri_b, mid) + _dot(tri_b, lo)


def _gla_decayed_operands(a_ref, g_cum, reverse):
    tb = a_ref.shape[0]
    n_chunks = tb // GLA_CHUNK
    q = a_ref[:, A_Q:A_Q + GLA_KWIDTH].astype(F32)
    k = a_ref[:, A_K:A_K + GLA_KWIDTH].astype(F32)
    total_rows = [c * GLA_CHUNK + (0 if reverse else GLA_CHUNK - 1) for c in range(n_chunks)]
    totals = [g_cum[r:r + 1, :] for r in total_rows]
    g_last = jnp.concatenate(
        [jnp.broadcast_to(t, (GLA_CHUNK, GLA_KWIDTH)) for t in totals], axis=0)
    dec_cols = [jnp.transpose(jnp.broadcast_to(jnp.exp(t), (GLA_DV, GLA_KWIDTH))) for t in totals]
    return {
        "q_dec": ((q * (GLA_DK ** -0.5)) * jnp.exp(g_cum)).astype(BF16),
        "k_dec": (k * jnp.exp(-g_cum)).astype(BF16),
        "k_state": (k * jnp.exp(g_last - g_cum)).astype(BF16),
        "v": a_ref[:, A_V:A_V + GLA_WIDTH],
        "dec_cols": dec_cols,
    }


def _gla_chunk_terms(qks_ref, v_ref, dec_ref, c, reverse):
    rows = pl.ds(c * GLA_CHUNK, GLA_CHUNK)
    q_c, k_c, ks_c, v_c = qks_ref[0, rows], qks_ref[1, rows], qks_ref[2, rows], v_ref[rows]
    zero_b = jnp.zeros((), BF16)
    k_head = lax.broadcasted_iota(jnp.int32, (GLA_CHUNK, GLA_KWIDTH), 1) // GLA_DK
    v_head = lax.broadcasted_iota(jnp.int32, (GLA_CHUNK, GLA_WIDTH), 1) // GLA_DV
    k_bd = jnp.concatenate([jnp.where(k_head == h, k_c, zero_b) for h in range(GLA_HEADS)], axis=0)
    v_bd = jnp.concatenate([jnp.where(v_head == h, v_c, zero_b) for h in range(GLA_HEADS)], axis=0)
    i = lax.broadcasted_iota(jnp.int32, (GLA_CHUNK, GLA_HEADS * GLA_CHUNK), 0)
    j = lax.broadcasted_iota(jnp.int32, (GLA_CHUNK, GLA_HEADS * GLA_CHUNK), 1) % GLA_CHUNK
    keep = (j >= i) if reverse else (j <= i)
    att = jnp.where(keep, _dot_nt(q_c, k_bd), 0.0).astype(BF16)
    kv = []
    for pair in range(GLA_HEADS // 2):
        both = _dot_tn(ks_c[:, pair * 2 * GLA_DK:(pair + 1) * 2 * GLA_DK],
                       v_c[:, pair * 2 * GLA_DV:(pair + 1) * 2 * GLA_DV])
        kv.append(both[:GLA_DK, :GLA_DV])
        kv.append(both[GLA_DK:, GLA_DV:])
    return {"q": q_c, "att": att, "v_bd": v_bd, "kv": kv, "dec": dec_ref[c]}


def _gla_chunk_step(terms, state, o_ref, c):
    zero_blk = jnp.zeros((GLA_DK, GLA_DV), BF16)
    rows = slice(c * GLA_CHUNK, (c + 1) * GLA_CHUNK)
    for pair in range(GLA_HEADS // 2):
        h0, h1 = 2 * pair, 2 * pair + 1
        j_lanes = slice(pair * 2 * GLA_CHUNK, (pair + 1) * 2 * GLA_CHUNK)
        k_lanes = slice(pair * 2 * GLA_DK, (pair + 1) * 2 * GLA_DK)
        v_lanes = slice(pair * 2 * GLA_DV, (pair + 1) * 2 * GLA_DV)
        s_bd = jnp.concatenate(
            [jnp.concatenate([state[h0].astype(BF16), zero_blk], axis=1),
             jnp.concatenate([zero_blk, state[h1].astype(BF16)], axis=1)], axis=0)
        lhs = jnp.concatenate([terms["att"][:, j_lanes], terms["q"][:, k_lanes]], axis=1)
        rhs = jnp.concatenate([terms["v_bd"][j_lanes, v_lanes], s_bd], axis=0)
        o_ref[rows, v_lanes] = _dot(lhs, rhs).astype(o_ref.dtype)
    dec = terms["dec"]
    return [dec[h * GLA_DK:(h + 1) * GLA_DK] * state[h] + terms["kv"][h] for h in range(GLA_HEADS)]


def _gla_kernel(af_ref, ab_ref, w2_ref, b2_ref, of_ref, ob_ref,
                qks_f, v_f, dec_f, st_f, qks_b, v_b, dec_b, st_b, *, tiles_per_seq):
    t = pl.program_id(0)

    @pl.when(t == 0)
    def _():
        for ref in (qks_f, v_f, dec_f, st_f, qks_b, v_b, dec_b, st_b):
            ref[...] = jnp.zeros_like(ref)

    n_chunks = af_ref.shape[0] // GLA_CHUNK
    fill = lax.rem(t, 2)
    use = 1 - fill
    dirs = (
        dict(a=af_ref, w2=w2_ref.at[0], b2=b2_ref.at[0], o=of_ref, st=st_f, reverse=False,
             qks=qks_f.at[use], v=v_f.at[use], dec=dec_f.at[use],
             qks_next=qks_f.at[fill], v_next=v_f.at[fill], dec_next=dec_f.at[fill]),
        dict(a=ab_ref, w2=w2_ref.at[1], b2=b2_ref.at[1], o=ob_ref, st=st_b, reverse=True,
             qks=qks_b.at[use], v=v_b.at[use], dec=dec_b.at[use],
             qks_next=qks_b.at[fill], v_next=v_b.at[fill], dec_next=dec_b.at[fill]),
    )
    restart = lax.rem(jnp.maximum(t - 1, 0), tiles_per_seq) == 0
    states = [[jnp.where(restart, 0.0, d["st"][h]) for h in range(GLA_HEADS)] for d in dirs]

    terms = [[None] * n_chunks for _ in dirs]

    def chunk_terms(chunks):
        for n, d in enumerate(dirs):
            for c in chunks:
                terms[n][c] = _gla_chunk_terms(d["qks"], d["v"], d["dec"], c, d["reverse"])

    half = n_chunks // 2
    log_a = [_gla_log_decay(d["a"], d["w2"], d["b2"]) for d in dirs]
    chunk_terms(range(half))
    g_cum = [_gla_cumsum(la, d["reverse"]) for la, d in zip(log_a, dirs)]
    chunk_terms(range(half, n_chunks))
    prepared = [_gla_decayed_operands(d["a"], g, d["reverse"]) for g, d in zip(g_cum, dirs)]
    for step in range(n_chunks):
        for n, d in enumerate(dirs):
            c = n_chunks - 1 - step if d["reverse"] else step
            states[n] = _gla_chunk_step(terms[n][c], states[n], d["o"], c)

    for n, d in enumerate(dirs):
        for h in range(GLA_HEADS):
            d["st"][h] = states[n][h]
        d["qks_next"][0] = prepared[n]["q_dec"]
        d["qks_next"][1] = prepared[n]["k_dec"]
        d["qks_next"][2] = prepared[n]["k_state"]
        d["v_next"][...] = prepared[n]["v"]
        for c in range(n_chunks):
            d["dec_next"][c] = prepared[n]["dec_cols"][c]


def _gla(a, w2p, b2, layer):
    bsz, seq, _ = a.shape
    tb = TB_GLA
    n = seq // tb
    n_chunks = tb // GLA_CHUNK
    total = bsz * n

    def prepared_tile(t):
        u = jnp.minimum(t, total - 1)
        return u // n, lax.rem(u, n)

    def recurrence_tile(t):
        u = jnp.maximum(t - 1, 0)
        return u // n, lax.rem(u, n)

    def fwd(tile):
        return lambda t: (tile(t)[0], tile(t)[1], 0)

    def bwd(tile):
        return lambda t: (tile(t)[0], n - 1 - tile(t)[1], 0)

    layer_block = lambda t: (layer, 0, 0, 0)
    direction_scratch = [
        pltpu.VMEM((2, 3, tb, GLA_KWIDTH), BF16),
        pltpu.VMEM((2, tb, GLA_WIDTH), BF16),
        pltpu.VMEM((2, n_chunks, GLA_KWIDTH, GLA_DV), F32),
        pltpu.VMEM((GLA_HEADS, GLA_DK, GLA_DV), F32),
    ]
    return pl.pallas_call(
        functools.partial(_gla_kernel, tiles_per_seq=n),
        grid=(total + 1,),
        in_specs=[
            pl.BlockSpec((None, tb, A_WIDTH), fwd(prepared_tile)),
            pl.BlockSpec((None, tb, A_WIDTH), bwd(prepared_tile)),
            pl.BlockSpec((None, 2, LR_PAD, GLA_KWIDTH), layer_block),
            pl.BlockSpec((None, 2, 1, GLA_KWIDTH), layer_block),
        ],
        out_specs=[
            pl.BlockSpec((None, tb, GLA_WIDTH), fwd(recurrence_tile)),
            pl.BlockSpec((None, tb, GLA_WIDTH), bwd(recurrence_tile)),
        ],
        out_shape=[jax.ShapeDtypeStruct((bsz, seq, GLA_WIDTH), BF16)] * 2,
        scratch_shapes=direction_scratch * 2,
        name="gla_scan",
        compiler_params=pltpu.CompilerParams(
            dimension_semantics=("arbitrary",),
            vmem_limit_bytes=VMEM_LIMIT_GLA),
    )(a, a, w2p, b2)


def _halo_valid(tm, seq_len):
    s0 = pl.program_id(1) * tm
    return s0, s0 > 0, s0 + tm < seq_len


def _mix_kernel(of_ref, ob_ref, bm_ref, xp_ref, hp_ref, hn_ref, x_ref, gn_ref, lng_ref, lnb_ref,
                wst_ref, sb_ref, pw_ref, ps_ref, wo_ref, out_ref, *, seq_len):
    tm = x_ref.shape[0]
    s0, prev_ok, next_ok = _halo_valid(tm, seq_len)

    o = of_ref[...].astype(F32) + ob_ref[...].astype(F32)
    heads = []
    for h in range(GLA_HEADS):
        o_h = o[:, h * GLA_DV:(h + 1) * GLA_DV]
        heads.append(o_h * lax.rsqrt(jnp.mean(o_h * o_h, axis=-1, keepdims=True) + EPS))
    gate = bm_ref[:, B_G:B_G + GLA_WIDTH].astype(F32)
    a_out = jnp.concatenate(heads, axis=-1) * gn_ref[...] * jax.nn.silu(gate)

    gu = jax.nn.gelu(bm_ref[:, B_U:B_U + SGU_WIDTH].astype(F32))
    gv = jax.nn.gelu(bm_ref[:, B_VS:B_VS + SGU_WIDTH].astype(F32))
    xc = gv - jnp.mean(gv, axis=-1, keepdims=True)
    v_ln = xc * lax.rsqrt(jnp.mean(xc * xc, axis=-1, keepdims=True) + EPS) * lng_ref[...] + lnb_ref[...]
    v_b = v_ln.astype(BF16)
    head_of_lane = lax.broadcasted_iota(jnp.int32, (SGU_CHUNK, SGU_WIDTH), 1) // SGU_HEAD_DIM
    b_parts = []
    for n in range(tm // SGU_CHUNK):
        rows = slice(n * SGU_CHUNK, (n + 1) * SGU_CHUNK)
        all_heads = _dot(wst_ref[...], v_b[rows])
        mixed = jnp.zeros((SGU_CHUNK, SGU_WIDTH), F32)
        for h in range(SGU_HEADS):
            mixed = mixed + jnp.where(head_of_lane == h, all_heads[h * SGU_CHUNK:(h + 1) * SGU_CHUNK], 0.0)
        b_parts.append(gu[rows] * (mixed + sb_ref[...]))
    b_out = jnp.concatenate(b_parts, axis=0)

    xp = xp_ref[...]
    ext = jnp.concatenate(
        [jnp.where(prev_ok, hp_ref[...], 0.0), xp, jnp.where(next_ok, hn_ref[...], 0.0)], axis=0)
    rows_ext = tm + 2 * HALO
    back = lambda t, d: pltpu.roll(t, d, 0)
    ahead = lambda t, d: pltpu.roll(t, rows_ext - d, 0)
    core = lambda t: t[HALO:HALO + tm]
    lo, hi = ext[:, :V7X_LANES], ext[:, V7X_LANES:]
    lo2, hi2 = lo + back(lo, 1), hi + back(hi, 1)
    lo4, hi4 = back(lo2, 1) + ahead(lo2, 1), back(hi2, 1) + ahead(hi2, 1)
    hi8 = back(hi4, 2) + ahead(hi4, 2)
    hi16 = back(hi8, 4) + ahead(hi8, 4)
    first_group = lax.broadcasted_iota(jnp.int32, (tm, V7X_LANES), 1) < POOL_GROUP_DIM
    win = jnp.concatenate([jnp.where(first_group, core(lo2), core(lo4)),
                           jnp.where(first_group, core(hi8), core(hi16))], axis=1)

    def half_window(rows):
        group = lax.broadcasted_iota(jnp.int32, (rows, POOL_WIDTH), 1) // POOL_GROUP_DIM
        return jnp.where(group == 0, 1, jnp.where(group == 1, 2, jnp.where(group == 2, 4, 8)))

    def edge_mean(r0):
        half = half_window(HALO)
        t = s0 + r0 + lax.broadcasted_iota(jnp.int32, (HALO, POOL_WIDTH), 0)
        cnt = jnp.minimum(t + half, seq_len) - jnp.maximum(t - half, 0)
        return win[r0:r0 + HALO] / cnt.astype(F32)

    inv_width = 0.5 / half_window(1).astype(F32)
    pooled = jnp.concatenate(
        [edge_mean(0), win[HALO:tm - HALO] * inv_width, edge_mean(tm - HALO)], axis=0)
    d = pooled - xp
    c_out = _dot(d.astype(BF16), pw_ref[...]) * ps_ref[...]

    mix = jnp.concatenate([a_out, b_out, c_out], axis=-1).astype(BF16)
    out_ref[...] = x_ref[...] + _dot(mix, wo_ref[...])


def _mix(o_f, o_b, bm, xp, x, p, layer):
    bsz, seq, _ = x.shape
    tm = TM_MIX
    hb = tm // HALO
    n_halo_blocks = seq // HALO
    tile = lambda b, j: (b, j, 0)
    layer_block = lambda b, j: (layer, 0, 0)
    return pl.pallas_call(
        functools.partial(_mix_kernel, seq_len=seq),
        grid=(bsz, seq // tm),
        in_specs=[
            pl.BlockSpec((None, tm, GLA_WIDTH), tile),
            pl.BlockSpec((None, tm, GLA_WIDTH), tile),
            pl.BlockSpec((None, tm, B_WIDTH), tile),
            pl.BlockSpec((None, tm, POOL_WIDTH), tile),
            pl.BlockSpec((None, HALO, POOL_WIDTH), lambda b, j: (b, jnp.maximum(j * hb - 1, 0), 0)),
            pl.BlockSpec((None, HALO, POOL_WIDTH),
                         lambda b, j: (b, jnp.minimum((j + 1) * hb, n_halo_blocks - 1), 0)),
            pl.BlockSpec((None, tm, D_MODEL), tile),
            pl.BlockSpec((None, 1, GLA_WIDTH), layer_block),
            pl.BlockSpec((None, 1, SGU_WIDTH), layer_block),
            pl.BlockSpec((None, 1, SGU_WIDTH), layer_block),
            pl.BlockSpec((None, SGU_HEADS * SGU_CHUNK, SGU_CHUNK), layer_block),
            pl.BlockSpec((None, SGU_CHUNK, SGU_WIDTH), layer_block),
            pl.BlockSpec((None, POOL_WIDTH, POOL_WIDTH), layer_block),
            pl.BlockSpec((None, 1, POOL_WIDTH), layer_block),
            pl.BlockSpec((None, D_MODEL, D_MODEL), layer_block),
        ],
        out_specs=pl.BlockSpec((None, tm, D_MODEL), tile),
        out_shape=jax.ShapeDtypeStruct((bsz, seq, D_MODEL), F32),
        name="mix_out",
        compiler_params=pltpu.CompilerParams(
            dimension_semantics=("arbitrary", "arbitrary"),
            vmem_limit_bytes=VMEM_LIMIT_MIX),
    )(o_f, o_b, bm, xp, xp, xp, x, p["gla_norm_g"], p["sgu_ln_g"], p["sgu_ln_b"], p["sgu_w"],
      p["sgu_b"], p["pool_w"], p["pool_scale"], p["w_o"])


def _ffn_kernel(x_ref, hp_ref, hn_ref, g_ref, w1_ref, cw_ref, cb_ref, w2_ref, nf_ref, out_ref,
                hid_ref, *, seq_len, final):
    tm = x_ref.shape[0]
    _, prev_ok, next_ok = _halo_valid(tm, seq_len)
    x = x_ref[...]
    x_ext = jnp.concatenate(
        [jnp.where(prev_ok, hp_ref[...], 0.0), x, jnp.where(next_ok, hn_ref[...], 0.0)], axis=0)
    h_ext = _rmsnorm(x_ext, g_ref[...])
    hb_ext = h_ext.astype(BF16)
    hb = h_ext[HALO:HALO + tm].astype(BF16)
    rows_ext = tm + 2 * HALO
    for c0 in range(0, D_FF, FFN_COL_CHUNK):
        cols = slice(c0, c0 + FFN_COL_CHUNK)
        a = _dot(hb_ext, w1_ref[:, cols])
        up = _dot(hb, w1_ref[:, D_FF + c0:D_FF + c0 + FFN_COL_CHUNK])
        conv = (pltpu.roll(a, 1, 0) * cw_ref[0:1, cols] + a * cw_ref[1:2, cols]
                + pltpu.roll(a, rows_ext - 1, 0) * cw_ref[2:3, cols] + cb_ref[:, cols])
        hid_ref[:, cols] = (jax.nn.silu(conv[HALO:HALO + tm]) * up).astype(BF16)
    y = x + _dot(hid_ref[...], w2_ref[...])
    if final:
        y = _rmsnorm(y, nf_ref[...])
    out_ref[...] = y


def _ffn(x, p, layer, final):
    bsz, seq, _ = x.shape
    tm = TM_FFN
    hb = tm // HALO
    n_halo_blocks = seq // HALO
    tile = lambda b, j: (b, j, 0)
    layer_block = lambda b, j: (layer, 0, 0)
    return pl.pallas_call(
        functools.partial(_ffn_kernel, seq_len=seq, final=final),
        grid=(bsz, seq // tm),
        in_specs=[
            pl.BlockSpec((None, tm, D_MODEL), tile),
            pl.BlockSpec((None, HALO, D_MODEL), lambda b, j: (b, jnp.maximum(j * hb - 1, 0), 0)),
            pl.BlockSpec((None, HALO, D_MODEL),
                         lambda b, j: (b, jnp.minimum((j + 1) * hb, n_halo_blocks - 1), 0)),
            pl.BlockSpec((None, 1, D_MODEL), layer_block),
            pl.BlockSpec((None, D_MODEL, 2 * D_FF), layer_block),
            pl.BlockSpec((None, 3, D_FF), layer_block),
            pl.BlockSpec((None, 1, D_FF), layer_block),
            pl.BlockSpec((None, D_FF, D_MODEL), layer_block),
            pl.BlockSpec((1, D_MODEL), lambda b, j: (0, 0)),
        ],
        out_specs=pl.BlockSpec((None, tm, D_MODEL), tile),
        out_shape=jax.ShapeDtypeStruct((bsz, seq, D_MODEL), F32),
        scratch_shapes=[pltpu.VMEM((tm, D_FF), BF16)],
        name="conv_glu_ffn",
        compiler_params=pltpu.CompilerParams(
            dimension_semantics=("arbitrary", "arbitrary"),
            vmem_limit_bytes=VMEM_LIMIT_FFN),
    )(x, x, x, p["norm2_g"], p["w_ffn_in"], p["conv_w"], p["conv_b"], p["w_ffn_out"], p["norm_f"])


def _prepare_params(norm1_g, w_in, gla_gate_w2, gla_gate_b, gla_norm_g, sgu_ln_g, sgu_ln_b,
                    sgu_w, sgu_b, pool_w, pool_scale, w_o, norm2_g, w_ffn_in, conv_w, conv_b,
                    w_ffn_out, norm_f):
    depth = w_in.shape[0]
    row = lambda t: t.reshape(depth, 1, t.shape[-1])
    c_qkv = 2 * GLA_KWIDTH + GLA_WIDTH
    c_g = c_qkv + GLA_WIDTH
    c_lr = c_g + 2 * GLA_GATE_RANK
    w_in_p = jnp.concatenate([
        w_in[:, :, :c_qkv], w_in[:, :, c_g:c_lr],
        jnp.zeros((depth, D_MODEL, LR_PAD - 2 * GLA_GATE_RANK), w_in.dtype),
        w_in[:, :, c_qkv:c_g], w_in[:, :, c_lr:]], axis=-1).astype(BF16)
    w2p = jnp.zeros((depth, 2, LR_PAD, GLA_KWIDTH), F32)
    for d in range(2):
        w2p = w2p.at[:, d, d * GLA_GATE_RANK:(d + 1) * GLA_GATE_RANK].set(gla_gate_w2[:, d])
    pool_bd = jnp.einsum("lgcd,gh->lgchd", pool_w, jnp.eye(len(POOL_WINDOWS), dtype=pool_w.dtype))
    return {
        "norm1_g": row(norm1_g),
        "w_in": w_in_p,
        "gate_w2": w2p.astype(BF16),
        "gate_b": gla_gate_b.reshape(depth, 2, 1, GLA_KWIDTH),
        "gla_norm_g": row(gla_norm_g),
        "sgu_ln_g": row(sgu_ln_g),
        "sgu_ln_b": row(sgu_ln_b),
        "sgu_w": sgu_w.reshape(depth, SGU_HEADS * SGU_CHUNK, SGU_CHUNK).astype(BF16),
        "sgu_b": jnp.repeat(jnp.swapaxes(sgu_b, 1, 2), SGU_HEAD_DIM, axis=2),
        "pool_w": pool_bd.reshape(depth, POOL_WIDTH, POOL_WIDTH).astype(BF16),
        "pool_scale": row(pool_scale),
        "w_o": w_o.astype(BF16),
        "norm2_g": row(norm2_g),
        "w_ffn_in": w_ffn_in.astype(BF16),
        "conv_w": conv_w,
        "conv_b": row(conv_b),
        "w_ffn_out": w_ffn_out.astype(BF16),
        "norm_f": norm_f.reshape(1, D_MODEL),
    }


def _trunk(x, p):
    depth = p["w_in"].shape[0]
    for layer in range(depth):
        a, bm, xp = _inproj(x, p["norm1_g"], p["w_in"], layer)
        o_f, o_b = _gla(a, p["gate_w2"], p["gate_b"], layer)
        x = _mix(o_f, o_b, bm, xp, x, p, layer)
        x = _ffn(x, p, layer, final=(layer == depth - 1))
    return x


def kernel(x_prompt, x_sample, norm1_g, w_in, gla_gate_w2, gla_gate_b, gla_norm_g, sgu_ln_g, sgu_ln_b, sgu_w, sgu_b, pool_w, pool_scale, w_o, norm2_g, w_ffn_in, conv_w, conv_b, w_ffn_out, norm_f):
    p = _prepare_params(norm1_g, w_in, gla_gate_w2, gla_gate_b, gla_norm_g, sgu_ln_g, sgu_ln_b,
                        sgu_w, sgu_b, pool_w, pool_scale, w_o, norm2_g, w_ffn_in, conv_w, conv_b,
                        w_ffn_out, norm_f)
    return (_trunk(x_prompt, p), _trunk(x_sample, p))
```

```python
import functools

import jax
import jax.numpy as jnp
from jax import lax
from jax.experimental import pallas as pl
from jax.experimental.pallas import tpu as pltpu

F32 = jnp.float32
BF16 = jnp.bfloat16

D_MODEL = 1024
DEPTH = 4
GLA_HEADS = 4
GLA_WIDTH = 512
GLA_DV = 128
GLA_DK = 64
GLA_KWIDTH = 256
GLA_GATE_RANK = 16
GLA_GATE_TAU = 16.0
GLA_CHUNK = 64
SGU_HEADS = 4
SGU_WIDTH = 256
SGU_HEAD_DIM = 64
SGU_CHUNK = 128
POOL_WINDOWS = (2, 4, 8, 16)
POOL_WIDTH = 256
POOL_GROUP_DIM = 64
POOL_HALO = max(POOL_WINDOWS) // 2
D_FF = 2816
EPS = 1e-6

V7X_LANES = 128
V7X_SUBLANES = 8
V7X_VMEM_BYTES = 64 * 1024 * 1024

LR_PAD = V7X_LANES
A_WIDTH = 2 * GLA_KWIDTH + GLA_WIDTH + LR_PAD
B_WIDTH = GLA_WIDTH + 2 * SGU_WIDTH
A_Q, A_K, A_V, A_LR = 0, GLA_KWIDTH, 2 * GLA_KWIDTH, 2 * GLA_KWIDTH + GLA_WIDTH
B_G, B_U, B_VS = 0, GLA_WIDTH, GLA_WIDTH + SGU_WIDTH
IN_WIDTH_PADDED = A_WIDTH + B_WIDTH + POOL_WIDTH

HALO = V7X_SUBLANES

TM_PROJ = 1024
TB_GLA = 256
TM_MIX = 1024
TM_FFN = 1024
FFN_COL_CHUNK = 256

VMEM_LIMIT_PROJ = 40 * 1024 * 1024
VMEM_LIMIT_GLA = 32 * 1024 * 1024
VMEM_LIMIT_MIX = 48 * 1024 * 1024
VMEM_LIMIT_FFN = 56 * 1024 * 1024


def _rmsnorm(x, g):
    return x * lax.rsqrt(jnp.mean(x * x, axis=-1, keepdims=True) + EPS) * g


def _dot(a, b):
    return jnp.dot(a, b, preferred_element_type=F32)


def _dot_nt(a, b):
    return lax.dot_general(a, b, (((1,), (1,)), ((), ())), preferred_element_type=F32)


def _dot_tn(a, b):
    return lax.dot_general(a, b, (((0,), (0,)), ((), ())), preferred_element_type=F32)


def _inproj_kernel(x_ref, g_ref, w_ref, a_ref, b_ref, xp_ref):
    hb = _rmsnorm(x_ref[...], g_ref[...]).astype(BF16)
    a_ref[...] = _dot(hb, w_ref[:, :A_WIDTH]).astype(BF16)
    b_ref[...] = _dot(hb, w_ref[:, A_WIDTH:A_WIDTH + B_WIDTH]).astype(BF16)
    xp_ref[...] = _dot(hb, w_ref[:, A_WIDTH + B_WIDTH:])


def _inproj(x, norm_g, w_in, layer):
    bsz, seq, _ = x.shape
    tm = TM_PROJ
    layer_block = lambda b, j: (layer, 0, 0)
    return pl.pallas_call(
        _inproj_kernel,
        grid=(bsz, seq // tm),
        in_specs=[
            pl.BlockSpec((None, tm, D_MODEL), lambda b, j: (b, j, 0)),
            pl.BlockSpec((None, 1, D_MODEL), layer_block),
            pl.BlockSpec((None, D_MODEL, IN_WIDTH_PADDED), layer_block),
        ],
        out_specs=[
            pl.BlockSpec((None, tm, A_WIDTH), lambda b, j: (b, j, 0)),
            pl.BlockSpec((None, tm, B_WIDTH), lambda b, j: (b, j, 0)),
            pl.BlockSpec((None, tm, POOL_WIDTH), lambda b, j: (b, j, 0)),
        ],
        out_shape=[
            jax.ShapeDtypeStruct((bsz, seq, A_WIDTH), BF16),
            jax.ShapeDtypeStruct((bsz, seq, B_WIDTH), BF16),
            jax.ShapeDtypeStruct((bsz, seq, POOL_WIDTH), F32),
        ],
        name="inproj",
        compiler_params=pltpu.CompilerParams(
            dimension_semantics=("arbitrary", "arbitrary"),
            vmem_limit_bytes=VMEM_LIMIT_PROJ),
    )(x, norm_g, w_in)


def _log_sigmoid(z):
    return jnp.minimum(z, 0.0) - jnp.log(1.0 + jnp.exp(-jnp.abs(z)))


def _split3_bf16(x):
    hi = x.astype(BF16)
    r1 = x - hi.astype(F32)
    mid = r1.astype(BF16)
    lo = (r1 - mid.astype(F32)).astype(BF16)
    return hi, mid, lo


def _gla_log_decay(a_ref, w2_ref, b2_ref):
    z = _dot(a_ref[:, A_LR:A_LR + LR_PAD], w2_ref[...]) + b2_ref[...]
    return _log_sigmoid(z) * (1.0 / GLA_GATE_TAU)


def _gla_cumsum(log_a, reverse):
    tb = log_a.shape[0]
    row = lax.broadcasted_iota(jnp.int32, (tb, tb), 0)
    col = lax.broadcasted_iota(jnp.int32, (tb, tb), 1)
    same_chunk = (row // GLA_CHUNK) == (col // GLA_CHUNK)
    tri = same_chunk & ((col >= row) if reverse else (col <= row))
    tri_b = jnp.where(tri, 1.0, 0.0).astype(BF16)
    hi, mid, lo = _split3_bf16(log_a)
    return _dot(tri_b, hi) + _dot(tri_b, mid) + _dot(tri_b, lo)


def _gla_decayed_operands(a_ref, g_cum, reverse):
    tb = a_ref.shape[0]
    n_chunks = tb // GLA_CHUNK
    q = a_ref[:, A_Q:A_Q + GLA_KWIDTH].astype(F32)
    k = a_ref[:, A_K:A_K + GLA_KWIDTH].astype(F32)
    total_rows = [c * GLA_CHUNK + (0 if reverse else GLA_CHUNK - 1) for c in range(n_chunks)]
    totals = [g_cum[r:r + 1, :] for r in total_rows]
    g_last = jnp.concatenate(
        [jnp.broadcast_to(t, (GLA_CHUNK, GLA_KWIDTH)) for t in totals], axis=0)
    dec_cols = [jnp.transpose(jnp.broadcast_to(jnp.exp(t), (GLA_DV, GLA_KWIDTH))) for t in totals]
    return {
        "q_dec": ((q * (GLA_DK ** -0.5)) * jnp.exp(g_cum)).astype(BF16),
        "k_dec": (k * jnp.exp(-g_cum)).astype(BF16),
        "k_state": (k * jnp.exp(g_last - g_cum)).astype(BF16),
        "v": a_ref[:, A_V:A_V + GLA_WIDTH],
        "dec_cols": dec_cols,
    }


def _gla_chunk_terms(qks_ref, v_ref, dec_ref, c, reverse):
    rows = pl.ds(c * GLA_CHUNK, GLA_CHUNK)
    q_c, k_c, ks_c, v_c = qks_ref[0, rows], qks_ref[1, rows], qks_ref[2, rows], v_ref[rows]
    zero_b = jnp.zeros((), BF16)
    k_head = lax.broadcasted_iota(jnp.int32, (GLA_CHUNK, GLA_KWIDTH), 1) // GLA_DK
    v_head = lax.broadcasted_iota(jnp.int32, (GLA_CHUNK, GLA_WIDTH), 1) // GLA_DV
    k_bd = jnp.concatenate([jnp.where(k_head == h, k_c, zero_b) for h in range(GLA_HEADS)], axis=0)
    v_bd = jnp.concatenate([jnp.where(v_head == h, v_c, zero_b) for h in range(GLA_HEADS)], axis=0)
    i = lax.broadcasted_iota(jnp.int32, (GLA_CHUNK, GLA_HEADS * GLA_CHUNK), 0)
    j = lax.broadcasted_iota(jnp.int32, (GLA_CHUNK, GLA_HEADS * GLA_CHUNK), 1) % GLA_CHUNK
    keep = (j >= i) if reverse else (j <= i)
    att = jnp.where(keep, _dot_nt(q_c, k_bd), 0.0).astype(BF16)
    kv = []
    for pair in range(GLA_HEADS // 2):
        both = _dot_tn(ks_c[:, pair * 2 * GLA_DK:(pair + 1) * 2 * GLA_DK],
                       v_c[:, pair * 2 * GLA_DV:(pair + 1) * 2 * GLA_DV])
        kv.append(both[:GLA_DK, :GLA_DV])
        kv.append(both[GLA_DK:, GLA_DV:])
    return {"q": q_c, "att": att, "v_bd": v_bd, "kv": kv, "dec": dec_ref[c]}


def _gla_chunk_step(terms, state, o_ref, c):
    zero_blk = jnp.zeros((GLA_DK, GLA_DV), BF16)
    rows = slice(c * GLA_CHUNK, (c + 1) * GLA_CHUNK)
    for pair in range(GLA_HEADS // 2):
        h0, h1 = 2 * pair, 2 * pair + 1
        j_lanes = slice(pair * 2 * GLA_CHUNK, (pair + 1) * 2 * GLA_CHUNK)
        k_lanes = slice(pair * 2 * GLA_DK, (pair + 1) * 2 * GLA_DK)
        v_lanes = slice(pair * 2 * GLA_DV, (pair + 1) * 2 * GLA_DV)
        s_bd = jnp.concatenate(
            [jnp.concatenate([state[h0].astype(BF16), zero_blk], axis=1),
             jnp.concatenate([zero_blk, state[h1].astype(BF16)], axis=1)], axis=0)
        lhs = jnp.concatenate([terms["att"][:, j_lanes], terms["q"][:, k_lanes]], axis=1)
        rhs = jnp.concatenate([terms["v_bd"][j_lanes, v_lanes], s_bd], axis=0)
        o_ref[rows, v_lanes] = _dot(lhs, rhs).astype(o_ref.dtype)
    dec = terms["dec"]
    return [dec[h * GLA_DK:(h + 1) * GLA_DK] * state[h] + terms["kv"][h] for h in range(GLA_HEADS)]


def _gla_kernel(af_ref, ab_ref, w2_ref, b2_ref, of_ref, ob_ref,
                qks_f, v_f, dec_f, st_f, qks_b, v_b, dec_b, st_b, *, tiles_per_seq):
    t = pl.program_id(0)

    @pl.when(t == 0)
    def _():
        for ref in (qks_f, v_f, dec_f, st_f, qks_b, v_b, dec_b, st_b):
            ref[...] = jnp.zeros_like(ref)

    n_chunks = af_ref.shape[0] // GLA_CHUNK
    fill = lax.rem(t, 2)
    use = 1 - fill
    dirs = (
        dict(a=af_ref, w2=w2_ref.at[0], b2=b2_ref.at[0], o=of_ref, st=st_f, reverse=False,
             qks=qks_f.at[use], v=v_f.at[use], dec=dec_f.at[use],
             qks_next=qks_f.at[fill], v_next=v_f.at[fill], dec_next=dec_f.at[fill]),
        dict(a=ab_ref, w2=w2_ref.at[1], b2=b2_ref.at[1], o=ob_ref, st=st_b, reverse=True,
             qks=qks_b.at[use], v=v_b.at[use], dec=dec_b.at[use],
             qks_next=qks_b.at[fill], v_next=v_b.at[fill], dec_next=dec_b.at[fill]),
    )
    restart = lax.rem(jnp.maximum(t - 1, 0), tiles_per_seq) == 0
    states = [[jnp.where(restart, 0.0, d["st"][h]) for h in range(GLA_HEADS)] for d in dirs]

    terms = [[None] * n_chunks for _ in dirs]

    def chunk_terms(chunks):
        for n, d in enumerate(dirs):
            for c in chunks:
                terms[n][c] = _gla_chunk_terms(d["qks"], d["v"], d["dec"], c, d["reverse"])

    half = n_chunks // 2
    log_a = [_gla_log_decay(d["a"], d["w2"], d["b2"]) for d in dirs]
    chunk_terms(range(half))
    g_cum = [_gla_cumsum(la, d["reverse"]) for la, d in zip(log_a, dirs)]
    chunk_terms(range(half, n_chunks))
    prepared = [_gla_decayed_operands(d["a"], g, d["reverse"]) for g, d in zip(g_cum, dirs)]
    for step in range(n_chunks):
        for n, d in enumerate(dirs):
            c = n_chunks - 1 - step if d["reverse"] else step
            states[n] = _gla_chunk_step(terms[n][c], states[n], d["o"], c)

    for n, d in enumerate(dirs):
        for h in range(GLA_HEADS):
            d["st"][h] = states[n][h]
        d["qks_next"][0] = prepared[n]["q_dec"]
        d["qks_next"][1] = prepared[n]["k_dec"]
        d["qks_next"][2] = prepared[n]["k_state"]
        d["v_next"][...] = prepared[n]["v"]
        for c in range(n_chunks):
            d["dec_next"][c] = prepared[n]["dec_cols"][c]


def _gla(a, w2p, b2, layer):
    bsz, seq, _ = a.shape
    tb = TB_GLA
    n = seq // tb
    n_chunks = tb // GLA_CHUNK
    total = bsz * n

    def prepared_tile(t):
        u = jnp.minimum(t, total - 1)
        return u // n, lax.rem(u, n)

    def recurrence_tile(t):
        u = jnp.maximum(t - 1, 0)
        return u // n, lax.rem(u, n)

    def fwd(tile):
        return lambda t: (tile(t)[0], tile(t)[1], 0)

    def bwd(tile):
        return lambda t: (tile(t)[0], n - 1 - tile(t)[1], 0)

    layer_block = lambda t: (layer, 0, 0, 0)
    direction_scratch = [
        pltpu.VMEM((2, 3, tb, GLA_KWIDTH), BF16),
        pltpu.VMEM((2, tb, GLA_WIDTH), BF16),
        pltpu.VMEM((2, n_chunks, GLA_KWIDTH, GLA_DV), F32),
        pltpu.VMEM((GLA_HEADS, GLA_DK, GLA_DV), F32),
    ]
    return pl.pallas_call(
        functools.partial(_gla_kernel, tiles_per_seq=n),
        grid=(total + 1,),
        in_specs=[
            pl.BlockSpec((None, tb, A_WIDTH), fwd(prepared_tile)),
            pl.BlockSpec((None, tb, A_WIDTH), bwd(prepared_tile)),
            pl.BlockSpec((None, 2, LR_PAD, GLA_KWIDTH), layer_block),
            pl.BlockSpec((None, 2, 1, GLA_KWIDTH), layer_block),
        ],
        out_specs=[
            pl.BlockSpec((None, tb, GLA_WIDTH), fwd(recurrence_tile)),
            pl.BlockSpec((None, tb, GLA_WIDTH), bwd(recurrence_tile)),
        ],
        out_shape=[jax.ShapeDtypeStruct((bsz, seq, GLA_WIDTH), BF16)] * 2,
        scratch_shapes=direction_scratch * 2,
        name="gla_scan",
        compiler_params=pltpu.CompilerParams(
            dimension_semantics=("arbitrary",),
            vmem_limit_bytes=VMEM_LIMIT_GLA),
    )(a, a, w2p, b2)


def _halo_valid(tm, seq_len):
    s0 = pl.program_id(1) * tm
    return s0, s0 > 0, s0 + tm < seq_len


def _gla_gated(of_ref, ob_ref, bm_ref, gn_ref):
    o = of_ref[...].astype(F32) + ob_ref[...].astype(F32)
    heads = []
    for h in range(GLA_HEADS):
        o_h = o[:, h * GLA_DV:(h + 1) * GLA_DV]
        heads.append(o_h * lax.rsqrt(jnp.mean(o_h * o_h, axis=-1, keepdims=True) + EPS))
    gate = bm_ref[:, B_G:B_G + GLA_WIDTH].astype(F32)
    return jnp.concatenate(heads, axis=-1) * gn_ref[...] * jax.nn.silu(gate)


def _spatial_gating(bm_ref, lng_ref, lnb_ref, wst_ref, sb_ref):
    tm = bm_ref.shape[0]
    gu = jax.nn.gelu(bm_ref[:, B_U:B_U + SGU_WIDTH].astype(F32))
    gv = jax.nn.gelu(bm_ref[:, B_VS:B_VS + SGU_WIDTH].astype(F32))
    xc = gv - jnp.mean(gv, axis=-1, keepdims=True)
    v_ln = xc * lax.rsqrt(jnp.mean(xc * xc, axis=-1, keepdims=True) + EPS) * lng_ref[...] + lnb_ref[...]
    v_b = v_ln.astype(BF16)
    head_of_lane = lax.broadcasted_iota(jnp.int32, (SGU_CHUNK, SGU_WIDTH), 1) // SGU_HEAD_DIM
    parts = []
    for n in range(tm // SGU_CHUNK):
        rows = slice(n * SGU_CHUNK, (n + 1) * SGU_CHUNK)
        all_heads = _dot(wst_ref[...], v_b[rows])
        mixed = jnp.zeros((SGU_CHUNK, SGU_WIDTH), F32)
        for h in range(SGU_HEADS):
            mixed = mixed + jnp.where(head_of_lane == h, all_heads[h * SGU_CHUNK:(h + 1) * SGU_CHUNK], 0.0)
        parts.append(gu[rows] * (mixed + sb_ref[...]))
    return jnp.concatenate(parts, axis=0)


def _pool_mixer(xp_ref, hp_ref, hn_ref, pw_ref, ps_ref, s0, seq_len):
    tm = xp_ref.shape[0]
    xp = xp_ref[...]
    ext = jnp.concatenate(
        [jnp.where(s0 > 0, hp_ref[...], 0.0), xp, jnp.where(s0 + tm < seq_len, hn_ref[...], 0.0)], axis=0)
    rows_ext = tm + 2 * HALO
    back = lambda t, d: pltpu.roll(t, d, 0)
    ahead = lambda t, d: pltpu.roll(t, rows_ext - d, 0)
    core = lambda t: t[HALO:HALO + tm]
    lo, hi = ext[:, :V7X_LANES], ext[:, V7X_LANES:]
    lo2, hi2 = lo + back(lo, 1), hi + back(hi, 1)
    lo4, hi4 = back(lo2, 1) + ahead(lo2, 1), back(hi2, 1) + ahead(hi2, 1)
    hi8 = back(hi4, 2) + ahead(hi4, 2)
    hi16 = back(hi8, 4) + ahead(hi8, 4)
    first_group = lax.broadcasted_iota(jnp.int32, (tm, V7X_LANES), 1) < POOL_GROUP_DIM
    win = jnp.concatenate([jnp.where(first_group, core(lo2), core(lo4)),
                           jnp.where(first_group, core(hi8), core(hi16))], axis=1)

    def half_window(rows):
        group = lax.broadcasted_iota(jnp.int32, (rows, POOL_WIDTH), 1) // POOL_GROUP_DIM
        return jnp.where(group == 0, 1, jnp.where(group == 1, 2, jnp.where(group == 2, 4, 8)))

    def edge_mean(r0):
        half = half_window(HALO)
        t = s0 + r0 + lax.broadcasted_iota(jnp.int32, (HALO, POOL_WIDTH), 0)
        cnt = jnp.minimum(t + half, seq_len) - jnp.maximum(t - half, 0)
        return win[r0:r0 + HALO] / cnt.astype(F32)

    inv_width = 0.5 / half_window(1).astype(F32)
    pooled = jnp.concatenate(
        [edge_mean(0), win[HALO:tm - HALO] * inv_width, edge_mean(tm - HALO)], axis=0)
    return _dot((pooled - xp).astype(BF16), pw_ref[...]) * ps_ref[...]


def _mix_kernel(of_ref, ob_ref, bm_ref, xp_ref, hp_ref, hn_ref, x_ref, gn_ref, lng_ref, lnb_ref,
                wst_ref, sb_ref, pw_ref, ps_ref, wo_ref, out_ref, mixed_ref, *,
                seq_len, tiles_per_seq, n_tiles):
    tm = x_ref.shape[0]
    t = pl.program_id(0)

    @pl.when(t == 0)
    def _():
        mixed_ref[...] = jnp.zeros_like(mixed_ref)

    fill = lax.rem(t, 2)
    use = 1 - fill
    s0 = lax.rem(jnp.minimum(t, n_tiles - 1), tiles_per_seq) * tm

    n_groups = 4
    gw = D_MODEL // n_groups

    def project(g):
        cols = slice(g * gw, (g + 1) * gw)
        out_ref[:, cols] = x_ref[:, cols] + _dot(mixed_ref[use], wo_ref[:, cols])

    project(0)
    a_out = _gla_gated(of_ref, ob_ref, bm_ref, gn_ref)
    project(1)
    b_out = _spatial_gating(bm_ref, lng_ref, lnb_ref, wst_ref, sb_ref)
    project(2)
    c_out = _pool_mixer(xp_ref, hp_ref, hn_ref, pw_ref, ps_ref, s0, seq_len)
    project(3)
    mixed_ref[fill] = jnp.concatenate([a_out, b_out, c_out], axis=-1).astype(BF16)


def _mix(o_f, o_b, bm, xp, x, p, layer):
    bsz, seq, _ = x.shape
    tm = TM_MIX
    n = seq // tm
    total = bsz * n
    hb = tm // HALO
    n_halo_blocks = seq // HALO

    def mixed_tile(t):
        u = jnp.minimum(t, total - 1)
        return u // n, lax.rem(u, n)

    def projected_tile(t):
        u = jnp.maximum(t - 1, 0)
        return u // n, lax.rem(u, n)

    def tile(which):
        return lambda t: (which(t)[0], which(t)[1], 0)

    def halo_before(t):
        b, j = mixed_tile(t)
        return b, jnp.maximum(j * hb - 1, 0), 0

    def halo_after(t):
        b, j = mixed_tile(t)
        return b, jnp.minimum((j + 1) * hb, n_halo_blocks - 1), 0

    layer_block = lambda t: (layer, 0, 0)
    return pl.pallas_call(
        functools.partial(_mix_kernel, seq_len=seq, tiles_per_seq=n, n_tiles=total),
        grid=(total + 1,),
        in_specs=[
            pl.BlockSpec((None, tm, GLA_WIDTH), tile(mixed_tile)),
            pl.BlockSpec((None, tm, GLA_WIDTH), tile(mixed_tile)),
            pl.BlockSpec((None, tm, B_WIDTH), tile(mixed_tile)),
            pl.BlockSpec((None, tm, POOL_WIDTH), tile(mixed_tile)),
            pl.BlockSpec((None, HALO, POOL_WIDTH), halo_before),
            pl.BlockSpec((None, HALO, POOL_WIDTH), halo_after),
            pl.BlockSpec((None, tm, D_MODEL), tile(projected_tile)),
            pl.BlockSpec((None, 1, GLA_WIDTH), layer_block),
            pl.BlockSpec((None, 1, SGU_WIDTH), layer_block),
            pl.BlockSpec((None, 1, SGU_WIDTH), layer_block),
            pl.BlockSpec((None, SGU_HEADS * SGU_CHUNK, SGU_CHUNK), layer_block),
            pl.BlockSpec((None, SGU_CHUNK, SGU_WIDTH), layer_block),
            pl.BlockSpec((None, POOL_WIDTH, POOL_WIDTH), layer_block),
            pl.BlockSpec((None, 1, POOL_WIDTH), layer_block),
            pl.BlockSpec((None, D_MODEL, D_MODEL), layer_block),
        ],
        out_specs=pl.BlockSpec((None, tm, D_MODEL), tile(projected_tile)),
        out_shape=jax.ShapeDtypeStruct((bsz, seq, D_MODEL), F32),
        scratch_shapes=[pltpu.VMEM((2, tm, D_MODEL), BF16)],
        name="mix_out",
        compiler_params=pltpu.CompilerParams(
            dimension_semantics=("arbitrary",),
            vmem_limit_bytes=VMEM_LIMIT_MIX),
    )(o_f, o_b, bm, xp, xp, xp, x, p["gla_norm_g"], p["sgu_ln_g"], p["sgu_ln_b"], p["sgu_w"],
      p["sgu_b"], p["pool_w"], p["pool_scale"], p["w_o"])


def _ffn_kernel(x_ref, hp_ref, hn_ref, g_ref, w1_ref, cw_ref, cb_ref, w2_ref, nf_ref, out_ref,
                hid_ref, *, seq_len, final):
    tm = x_ref.shape[0]
    _, prev_ok, next_ok = _halo_valid(tm, seq_len)
    x = x_ref[...]
    x_ext = jnp.concatenate(
        [jnp.where(prev_ok, hp_ref[...], 0.0), x, jnp.where(next_ok, hn_ref[...], 0.0)], axis=0)
    h_ext = _rmsnorm(x_ext, g_ref[...])
    hb_ext = h_ext.astype(BF16)
    hb = h_ext[HALO:HALO + tm].astype(BF16)
    rows_ext = tm + 2 * HALO
    for c0 in range(0, D_FF, FFN_COL_CHUNK):
        cols = slice(c0, c0 + FFN_COL_CHUNK)
        a = _dot(hb_ext, w1_ref[:, cols])
        up = _dot(hb, w1_ref[:, D_FF + c0:D_FF + c0 + FFN_COL_CHUNK])
        conv = (pltpu.roll(a, 1, 0) * cw_ref[0:1, cols] + a * cw_ref[1:2, cols]
                + pltpu.roll(a, rows_ext - 1, 0) * cw_ref[2:3, cols] + cb_ref[:, cols])
        hid_ref[:, cols] = (jax.nn.silu(conv[HALO:HALO + tm]) * up).astype(BF16)
    y = x + _dot(hid_ref[...], w2_ref[...])
    if final:
        y = _rmsnorm(y, nf_ref[...])
    out_ref[...] = y


def _ffn(x, p, layer, final):
    bsz, seq, _ = x.shape
    tm = TM_FFN
    hb = tm // HALO
    n_halo_blocks = seq // HALO
    tile = lambda b, j: (b, j, 0)
    layer_block = lambda b, j: (layer, 0, 0)
    return pl.pallas_call(
        functools.partial(_ffn_kernel, seq_len=seq, final=final),
        grid=(bsz, seq // tm),
        in_specs=[
            pl.BlockSpec((None, tm, D_MODEL), tile),
            pl.BlockSpec((None, HALO, D_MODEL), lambda b, j: (b, jnp.maximum(j * hb - 1, 0), 0)),
            pl.BlockSpec((None, HALO, D_MODEL),
                         lambda b, j: (b, jnp.minimum((j + 1) * hb, n_halo_blocks - 1), 0)),
            pl.BlockSpec((None, 1, D_MODEL), layer_block),
            pl.BlockSpec((None, D_MODEL, 2 * D_FF), layer_block, pipeline_mode=pl.Buffered(1)),
            pl.BlockSpec((None, 3, D_FF), layer_block),
            pl.BlockSpec((None, 1, D_FF), layer_block),
            pl.BlockSpec((None, D_FF, D_MODEL), layer_block, pipeline_mode=pl.Buffered(1)),
            pl.BlockSpec((1, D_MODEL), lambda b, j: (0, 0)),
        ],
        out_specs=pl.BlockSpec((None, tm, D_MODEL), tile),
        out_shape=jax.ShapeDtypeStruct((bsz, seq, D_MODEL), F32),
        scratch_shapes=[pltpu.VMEM((tm, D_FF), BF16)],
        name="conv_glu_ffn",
        compiler_params=pltpu.CompilerParams(
            dimension_semantics=("arbitrary", "arbitrary"),
            vmem_limit_bytes=VMEM_LIMIT_FFN),
    )(x, x, x, p["norm2_g"], p["w_ffn_in"], p["conv_w"], p["conv_b"], p["w_ffn_out"], p["norm_f"])


def _prepare_params(norm1_g, w_in, gla_gate_w2, gla_gate_b, gla_norm_g, sgu_ln_g, sgu_ln_b,
                    sgu_w, sgu_b, pool_w, pool_scale, w_o, norm2_g, w_ffn_in, conv_w, conv_b,
                    w_ffn_out, norm_f):
    depth = w_in.shape[0]
    row = lambda t: t.reshape(depth, 1, t.shape[-1])
    c_qkv = 2 * GLA_KWIDTH + GLA_WIDTH
    c_g = c_qkv + GLA_WIDTH
    c_lr = c_g + 2 * GLA_GATE_RANK
    w_in_p = jnp.concatenate([
        w_in[:, :, :c_qkv], w_in[:, :, c_g:c_lr],
        jnp.zeros((depth, D_MODEL, LR_PAD - 2 * GLA_GATE_RANK), w_in.dtype),
        w_in[:, :, c_qkv:c_g], w_in[:, :, c_lr:]], axis=-1).astype(BF16)
    w2p = jnp.zeros((depth, 2, LR_PAD, GLA_KWIDTH), F32)
    for d in range(2):
        w2p = w2p.at[:, d, d * GLA_GATE_RANK:(d + 1) * GLA_GATE_RANK].set(gla_gate_w2[:, d])
    pool_bd = jnp.einsum("lgcd,gh->lgchd", pool_w, jnp.eye(len(POOL_WINDOWS), dtype=pool_w.dtype))
    return {
        "norm1_g": row(norm1_g),
        "w_in": w_in_p,
        "gate_w2": w2p.astype(BF16),
        "gate_b": gla_gate_b.reshape(depth, 2, 1, GLA_KWIDTH),
        "gla_norm_g": row(gla_norm_g),
        "sgu_ln_g": row(sgu_ln_g),
        "sgu_ln_b": row(sgu_ln_b),
        "sgu_w": sgu_w.reshape(depth, SGU_HEADS * SGU_CHUNK, SGU_CHUNK).astype(BF16),
        "sgu_b": jnp.repeat(jnp.swapaxes(sgu_b, 1, 2), SGU_HEAD_DIM, axis=2),
        "pool_w": pool_bd.reshape(depth, POOL_WIDTH, POOL_WIDTH).astype(BF16),
        "pool_scale": row(pool_scale),
        "w_o": w_o.astype(BF16),
        "norm2_g": row(norm2_g),
        "w_ffn_in": w_ffn_in.astype(BF16),
        "conv_w": conv_w,
        "conv_b": row(conv_b),
        "w_ffn_out": w_ffn_out.astype(BF16),
        "norm_f": norm_f.reshape(1, D_MODEL),
    }


def _trunk(x, p):
    depth = p["w_in"].shape[0]
    for layer in range(depth):
        a, bm, xp = _inproj(x, p["norm1_g"], p["w_in"], layer)
        o_f, o_b = _gla(a, p["gate_w2"], p["gate_b"], layer)
        x = _mix(o_f, o_b, bm, xp, x, p, layer)
        x = _ffn(x, p, layer, final=(layer == depth - 1))
    return x


def kernel(x_prompt, x_sample, norm1_g, w_in, gla_gate_w2, gla_gate_b, gla_norm_g, sgu_ln_g, sgu_ln_b, sgu_w, sgu_b, pool_w, pool_scale, w_o, norm2_g, w_ffn_in, conv_w, conv_b, w_ffn_out, norm_f):
    p = _prepare_params(norm1_g, w_in, gla_gate_w2, gla_gate_b, gla_norm_g, sgu_ln_g, sgu_ln_b,
                        sgu_w, sgu_b, pool_w, pool_scale, w_o, norm2_g, w_ffn_in, conv_w, conv_b,
                        w_ffn_out, norm_f)
    return (_trunk(x_prompt, p), _trunk(x_sample, p))
```

```python
import functools

import jax
import jax.numpy as jnp
from jax import lax
from jax.experimental import pallas as pl
from jax.experimental.pallas import tpu as pltpu

F32 = jnp.float32
BF16 = jnp.bfloat16

D_MODEL = 1024
DEPTH = 4
GLA_HEADS = 4
GLA_WIDTH = 512
GLA_DV = 128
GLA_DK = 64
GLA_KWIDTH = 256
GLA_GATE_RANK = 16
GLA_GATE_TAU = 16.0
GLA_CHUNK = 64
SGU_HEADS = 4
SGU_WIDTH = 256
SGU_HEAD_DIM = 64
SGU_CHUNK = 128
POOL_WINDOWS = (2, 4, 8, 16)
POOL_WIDTH = 256
POOL_GROUP_DIM = 64
POOL_HALO = max(POOL_WINDOWS) // 2
D_FF = 2816
EPS = 1e-6

V7X_LANES = 128
V7X_SUBLANES = 8
V7X_VMEM_BYTES = 64 * 1024 * 1024

LR_PAD = V7X_LANES
A_WIDTH = 2 * GLA_KWIDTH + GLA_WIDTH + LR_PAD
B_WIDTH = GLA_WIDTH + 2 * SGU_WIDTH
A_Q, A_K, A_V, A_LR = 0, GLA_KWIDTH, 2 * GLA_KWIDTH, 2 * GLA_KWIDTH + GLA_WIDTH
B_G, B_U, B_VS = 0, GLA_WIDTH, GLA_WIDTH + SGU_WIDTH
IN_WIDTH_PADDED = A_WIDTH + B_WIDTH + POOL_WIDTH

HALO = V7X_SUBLANES

TM_PROJ = 1024
TB_GLA = 1024
TM_MIX = 1024
TM_FFN = 1024
FFN_COL_CHUNK = 256
PROJ_ROW_BLOCK = 256

VMEM_LIMIT_PROJ = 48 * 1024 * 1024
VMEM_LIMIT_GLA = 32 * 1024 * 1024
VMEM_LIMIT_MIX = 48 * 1024 * 1024
VMEM_LIMIT_FFN = 56 * 1024 * 1024


def _rmsnorm(x, g):
    return x * lax.rsqrt(jnp.mean(x * x, axis=-1, keepdims=True) + EPS) * g


def _dot(a, b):
    return jnp.dot(a, b, preferred_element_type=F32)


def _dot_nt(a, b):
    return lax.dot_general(a, b, (((1,), (1,)), ((), ())), preferred_element_type=F32)


def _dot_tn(a, b):
    return lax.dot_general(a, b, (((0,), (0,)), ((), ())), preferred_element_type=F32)


def _log_sigmoid(z):
    return jnp.minimum(z, 0.0) - jnp.log(1.0 + jnp.exp(-jnp.abs(z)))


def _chunk_prefix_sums(x):
    row_in_chunk = lax.broadcasted_iota(jnp.int32, x.shape, 0) % GLA_CHUNK
    shift = 1
    while shift < GLA_CHUNK:
        x = x + jnp.where(row_in_chunk >= shift, pltpu.roll(x, shift, 0), 0.0)
        shift *= 2
    return x


def _gate_stages(a_parts, w2_ref, b2_ref, qks_f_ref, qks_b_ref, dec_ref, r0, rb):
    rows = pl.ds(r0, rb)
    n_chunks = rb // GLA_CHUNK
    fwd, bwd = slice(0, GLA_KWIDTH), slice(GLA_KWIDTH, 2 * GLA_KWIDTH)
    val = {}

    def gates():
        z = _dot(a_parts["lr"].astype(BF16), w2_ref[...]) + b2_ref[...]
        val["log_a"] = _log_sigmoid(z) * (1.0 / GLA_GATE_TAU)

    def scan():
        val["prefix"] = _chunk_prefix_sums(val["log_a"])

    def totals():
        prefix = val["prefix"]
        last = [prefix[(c + 1) * GLA_CHUNK - 1:(c + 1) * GLA_CHUNK, :] for c in range(n_chunks)]
        val["total"] = jnp.concatenate(
            [jnp.broadcast_to(t, (GLA_CHUNK, 2 * GLA_KWIDTH)) for t in last], axis=0)
        dec_ref[pl.ds(r0 // GLA_CHUNK * HALO, n_chunks * HALO)] = jnp.concatenate(
            [jnp.broadcast_to(jnp.exp(t), (HALO, 2 * GLA_KWIDTH)) for t in last], axis=0)

    def emit(out_ref, g_cum, g_total):
        q = a_parts["q"] * (GLA_DK ** -0.5)
        k = a_parts["k"]
        out_ref[rows, 0:GLA_KWIDTH] = (q * jnp.exp(g_cum)).astype(BF16)
        out_ref[rows, GLA_KWIDTH:2 * GLA_KWIDTH] = (k * jnp.exp(-g_cum)).astype(BF16)
        out_ref[rows, 2 * GLA_KWIDTH:] = (k * jnp.exp(g_total - g_cum)).astype(BF16)

    def forward():
        emit(qks_f_ref, val["prefix"][:, fwd], val["total"][:, fwd])

    def backward():
        suffix = val["total"][:, bwd] - val["prefix"][:, bwd] + val["log_a"][:, bwd]
        emit(qks_b_ref, suffix, val["total"][:, bwd])

    return [gates, scan, totals, forward, backward]


def _inproj_kernel(x_ref, g_ref, w_ref, w2_ref, b2_ref, qks_f_ref, qks_b_ref, v_ref, dec_ref, b_ref,
                   xp_ref):
    tm = x_ref.shape[0]
    rb = PROJ_ROW_BLOCK
    n_blocks = tm // rb
    hb = [None] * n_blocks
    a_parts = [dict() for _ in range(n_blocks)]

    def normalize(r):
        hb[r] = _rmsnorm(x_ref[pl.ds(r * rb, rb)], g_ref[...]).astype(BF16)

    def project(r, c0, c1):
        return _dot(hb[r], w_ref[:, c0:c1])

    def gla_pieces(r):
        rows = pl.ds(r * rb, rb)

        def part(name, c0, c1):
            return lambda: a_parts[r].__setitem__(name, project(r, c0, c1))

        def values(c0, c1):
            def piece():
                v_ref[rows, c0 - A_V:c1 - A_V] = project(r, c0, c1).astype(BF16)
            return piece

        half_v = A_V + GLA_WIDTH // 2
        return [part("lr", A_LR, A_LR + LR_PAD), part("q", A_Q, A_K), part("k", A_K, A_V),
                values(A_V, half_v), values(half_v, A_LR)]

    def other_pieces(r):
        rows = pl.ds(r * rb, rb)

        def branch(c0, c1):
            def piece():
                b_ref[rows, c0:c1] = project(r, A_WIDTH + c0, A_WIDTH + c1).astype(BF16)
            return piece

        def pool():
            xp_ref[rows] = project(r, A_WIDTH + B_WIDTH, IN_WIDTH_PADDED)

        step = 2 * V7X_LANES
        return [branch(c, c + step) for c in range(0, B_WIDTH, step)] + [pool]

    normalize(0)
    for piece in gla_pieces(0):
        piece()
    for r in range(1, n_blocks):
        normalize(r)
    queue = []
    for r in range(n_blocks):
        if r + 1 < n_blocks:
            queue = gla_pieces(r + 1) + queue
        queue = queue + other_pieces(r)
        for stage in _gate_stages(a_parts[r], w2_ref, b2_ref, qks_f_ref, qks_b_ref, dec_ref, r * rb, rb):
            stage()
            for _ in range(2):
                if queue:
                    queue.pop(0)()
    for piece in queue:
        piece()


def _inproj(x, p, layer):
    bsz, seq, _ = x.shape
    tm = TM_PROJ
    tile = lambda b, j: (b, j, 0)
    layer_block = lambda b, j: (layer, 0, 0)
    return pl.pallas_call(
        _inproj_kernel,
        grid=(bsz, seq // tm),
        in_specs=[
            pl.BlockSpec((None, tm, D_MODEL), tile),
            pl.BlockSpec((None, 1, D_MODEL), layer_block),
            pl.BlockSpec((None, D_MODEL, IN_WIDTH_PADDED), layer_block),
            pl.BlockSpec((None, LR_PAD, 2 * GLA_KWIDTH), layer_block),
            pl.BlockSpec((None, 1, 2 * GLA_KWIDTH), layer_block),
        ],
        out_specs=[
            pl.BlockSpec((None, tm, 3 * GLA_KWIDTH), tile),
            pl.BlockSpec((None, tm, 3 * GLA_KWIDTH), tile),
            pl.BlockSpec((None, tm, GLA_WIDTH), tile),
            pl.BlockSpec((None, tm // GLA_CHUNK * HALO, 2 * GLA_KWIDTH), tile),
            pl.BlockSpec((None, tm, B_WIDTH), tile),
            pl.BlockSpec((None, tm, POOL_WIDTH), tile),
        ],
        out_shape=[
            jax.ShapeDtypeStruct((bsz, seq, 3 * GLA_KWIDTH), BF16),
            jax.ShapeDtypeStruct((bsz, seq, 3 * GLA_KWIDTH), BF16),
            jax.ShapeDtypeStruct((bsz, seq, GLA_WIDTH), BF16),
            jax.ShapeDtypeStruct((bsz, seq // GLA_CHUNK * HALO, 2 * GLA_KWIDTH), F32),
            jax.ShapeDtypeStruct((bsz, seq, B_WIDTH), BF16),
            jax.ShapeDtypeStruct((bsz, seq, POOL_WIDTH), F32),
        ],
        name="inproj",
        compiler_params=pltpu.CompilerParams(
            dimension_semantics=("arbitrary", "arbitrary"),
            vmem_limit_bytes=VMEM_LIMIT_PROJ),
    )(x, p["norm1_g"], p["w_in"], p["gate_w2"], p["gate_b"])


def _gla_chunk_terms(qks_ref, v_ref, dec_ref, lanes, c, reverse):
    rows = pl.ds(c * GLA_CHUNK, GLA_CHUNK)
    q_c = qks_ref[rows, 0:GLA_KWIDTH]
    k_c = qks_ref[rows, GLA_KWIDTH:2 * GLA_KWIDTH]
    ks_c = qks_ref[rows, 2 * GLA_KWIDTH:3 * GLA_KWIDTH]
    v_c = v_ref[rows]
    zero_b = jnp.zeros((), BF16)
    k_head = lax.broadcasted_iota(jnp.int32, (GLA_CHUNK, GLA_KWIDTH), 1) // GLA_DK
    v_head = lax.broadcasted_iota(jnp.int32, (GLA_CHUNK, GLA_WIDTH), 1) // GLA_DV
    k_bd = jnp.concatenate([jnp.where(k_head == h, k_c, zero_b) for h in range(GLA_HEADS)], axis=0)
    v_bd = jnp.concatenate([jnp.where(v_head == h, v_c, zero_b) for h in range(GLA_HEADS)], axis=0)
    i = lax.broadcasted_iota(jnp.int32, (GLA_CHUNK, GLA_HEADS * GLA_CHUNK), 0)
    j = lax.broadcasted_iota(jnp.int32, (GLA_CHUNK, GLA_HEADS * GLA_CHUNK), 1) % GLA_CHUNK
    keep = (j >= i) if reverse else (j <= i)
    att = jnp.where(keep, _dot_nt(q_c, k_bd), 0.0).astype(BF16)
    kv = []
    for pair in range(GLA_HEADS // 2):
        both = _dot_tn(ks_c[:, pair * 2 * GLA_DK:(pair + 1) * 2 * GLA_DK],
                       v_c[:, pair * 2 * GLA_DV:(pair + 1) * 2 * GLA_DV])
        kv.append(both[:GLA_DK, :GLA_DV])
        kv.append(both[GLA_DK:, GLA_DV:])
    dec_row = dec_ref[c * HALO:c * HALO + 1, lanes]
    dec = jnp.transpose(jnp.broadcast_to(dec_row, (GLA_DV, GLA_KWIDTH)))
    return {"q": q_c, "att": att, "v_bd": v_bd, "kv": kv, "dec": dec}


def _gla_chunk_step(terms, state, o_ref, c):
    zero_blk = jnp.zeros((GLA_DK, GLA_DV), BF16)
    rows = slice(c * GLA_CHUNK, (c + 1) * GLA_CHUNK)
    for pair in range(GLA_HEADS // 2):
        h0, h1 = 2 * pair, 2 * pair + 1
        j_lanes = slice(pair * 2 * GLA_CHUNK, (pair + 1) * 2 * GLA_CHUNK)
        k_lanes = slice(pair * 2 * GLA_DK, (pair + 1) * 2 * GLA_DK)
        v_lanes = slice(pair * 2 * GLA_DV, (pair + 1) * 2 * GLA_DV)
        s_bd = jnp.concatenate(
            [jnp.concatenate([state[h0].astype(BF16), zero_blk], axis=1),
             jnp.concatenate([zero_blk, state[h1].astype(BF16)], axis=1)], axis=0)
        lhs = jnp.concatenate([terms["att"][:, j_lanes], terms["q"][:, k_lanes]], axis=1)
        rhs = jnp.concatenate([terms["v_bd"][j_lanes, v_lanes], s_bd], axis=0)
        o_ref[rows, v_lanes] = _dot(lhs, rhs).astype(o_ref.dtype)
    dec = terms["dec"]
    return [dec[h * GLA_DK:(h + 1) * GLA_DK] * state[h] + terms["kv"][h] for h in range(GLA_HEADS)]


def _gla_kernel(qks_f, v_f, dec_f, qks_b, v_b, dec_b, of_ref, ob_ref, st_f, st_b):
    @pl.when(pl.program_id(1) == 0)
    def _():
        st_f[...] = jnp.zeros_like(st_f)
        st_b[...] = jnp.zeros_like(st_b)

    n_chunks = qks_f.shape[0] // GLA_CHUNK
    dirs = (
        dict(qks=qks_f, v=v_f, dec=dec_f, lanes=slice(0, GLA_KWIDTH), o=of_ref, st=st_f, reverse=False),
        dict(qks=qks_b, v=v_b, dec=dec_b, lanes=slice(GLA_KWIDTH, 2 * GLA_KWIDTH), o=ob_ref, st=st_b,
             reverse=True),
    )
    terms = [[_gla_chunk_terms(d["qks"], d["v"], d["dec"], d["lanes"], c, d["reverse"])
              for c in range(n_chunks)] for d in dirs]
    states = [[d["st"][h] for h in range(GLA_HEADS)] for d in dirs]
    for step in range(n_chunks):
        for n, d in enumerate(dirs):
            c = n_chunks - 1 - step if d["reverse"] else step
            states[n] = _gla_chunk_step(terms[n][c], states[n], d["o"], c)
    for n, d in enumerate(dirs):
        for h in range(GLA_HEADS):
            d["st"][h] = states[n][h]


def _gla(qks_f, qks_b, v, dec, layer):
    bsz, seq, _ = v.shape
    tb = TB_GLA
    n = seq // tb
    dec_rows = tb // GLA_CHUNK * HALO
    fwd = lambda b, i: (b, i, 0)
    bwd = lambda b, i: (b, n - 1 - i, 0)
    return pl.pallas_call(
        _gla_kernel,
        grid=(bsz, n),
        in_specs=[
            pl.BlockSpec((None, tb, 3 * GLA_KWIDTH), fwd),
            pl.BlockSpec((None, tb, GLA_WIDTH), fwd),
            pl.BlockSpec((None, dec_rows, 2 * GLA_KWIDTH), fwd),
            pl.BlockSpec((None, tb, 3 * GLA_KWIDTH), bwd),
            pl.BlockSpec((None, tb, GLA_WIDTH), bwd),
            pl.BlockSpec((None, dec_rows, 2 * GLA_KWIDTH), bwd),
        ],
        out_specs=[
            pl.BlockSpec((None, tb, GLA_WIDTH), fwd),
            pl.BlockSpec((None, tb, GLA_WIDTH), bwd),
        ],
        out_shape=[jax.ShapeDtypeStruct((bsz, seq, GLA_WIDTH), BF16)] * 2,
        scratch_shapes=[pltpu.VMEM((GLA_HEADS, GLA_DK, GLA_DV), F32)] * 2,
        name="gla_scan",
        compiler_params=pltpu.CompilerParams(
            dimension_semantics=("arbitrary", "arbitrary"),
            vmem_limit_bytes=VMEM_LIMIT_GLA),
    )(qks_f, v, dec, qks_b, v, dec)


def _halo_valid(tm, seq_len):
    s0 = pl.program_id(1) * tm
    return s0, s0 > 0, s0 + tm < seq_len


def _gla_gated(of_ref, ob_ref, bm_ref, gn_ref):
    o = of_ref[...].astype(F32) + ob_ref[...].astype(F32)
    heads = []
    for h in range(GLA_HEADS):
        o_h = o[:, h * GLA_DV:(h + 1) * GLA_DV]
        heads.append(o_h * lax.rsqrt(jnp.mean(o_h * o_h, axis=-1, keepdims=True) + EPS))
    gate = bm_ref[:, B_G:B_G + GLA_WIDTH].astype(F32)
    return jnp.concatenate(heads, axis=-1) * gn_ref[...] * jax.nn.silu(gate)


def _spatial_gating(bm_ref, lng_ref, lnb_ref, wst_ref, sb_ref):
    tm = bm_ref.shape[0]
    gu = jax.nn.gelu(bm_ref[:, B_U:B_U + SGU_WIDTH].astype(F32))
    gv = jax.nn.gelu(bm_ref[:, B_VS:B_VS + SGU_WIDTH].astype(F32))
    xc = gv - jnp.mean(gv, axis=-1, keepdims=True)
    v_ln = xc * lax.rsqrt(jnp.mean(xc * xc, axis=-1, keepdims=True) + EPS) * lng_ref[...] + lnb_ref[...]
    v_b = v_ln.astype(BF16)
    head_of_lane = lax.broadcasted_iota(jnp.int32, (SGU_CHUNK, SGU_WIDTH), 1) // SGU_HEAD_DIM
    parts = []
    for n in range(tm // SGU_CHUNK):
        rows = slice(n * SGU_CHUNK, (n + 1) * SGU_CHUNK)
        all_heads = _dot(wst_ref[...], v_b[rows])
        mixed = jnp.zeros((SGU_CHUNK, SGU_WIDTH), F32)
        for h in range(SGU_HEADS):
            mixed = mixed + jnp.where(head_of_lane == h, all_heads[h * SGU_CHUNK:(h + 1) * SGU_CHUNK], 0.0)
        parts.append(gu[rows] * (mixed + sb_ref[...]))
    return jnp.concatenate(parts, axis=0)


def _pool_mixer(xp_ref, hp_ref, hn_ref, pw_ref, ps_ref, s0, seq_len):
    tm = xp_ref.shape[0]
    xp = xp_ref[...]
    ext = jnp.concatenate(
        [jnp.where(s0 > 0, hp_ref[...], 0.0), xp, jnp.where(s0 + tm < seq_len, hn_ref[...], 0.0)], axis=0)
    rows_ext = tm + 2 * HALO
    back = lambda t, d: pltpu.roll(t, d, 0)
    ahead = lambda t, d: pltpu.roll(t, rows_ext - d, 0)
    core = lambda t: t[HALO:HALO + tm]
    lo, hi = ext[:, :V7X_LANES], ext[:, V7X_LANES:]
    lo2, hi2 = lo + back(lo, 1), hi + back(hi, 1)
    lo4, hi4 = back(lo2, 1) + ahead(lo2, 1), back(hi2, 1) + ahead(hi2, 1)
    hi8 = back(hi4, 2) + ahead(hi4, 2)
    hi16 = back(hi8, 4) + ahead(hi8, 4)
    first_group = lax.broadcasted_iota(jnp.int32, (tm, V7X_LANES), 1) < POOL_GROUP_DIM
    win = jnp.concatenate([jnp.where(first_group, core(lo2), core(lo4)),
                           jnp.where(first_group, core(hi8), core(hi16))], axis=1)

    def half_window(rows):
        group = lax.broadcasted_iota(jnp.int32, (rows, POOL_WIDTH), 1) // POOL_GROUP_DIM
        return jnp.where(group == 0, 1, jnp.where(group == 1, 2, jnp.where(group == 2, 4, 8)))

    def edge_mean(r0):
        half = half_window(HALO)
        t = s0 + r0 + lax.broadcasted_iota(jnp.int32, (HALO, POOL_WIDTH), 0)
        cnt = jnp.minimum(t + half, seq_len) - jnp.maximum(t - half, 0)
        return win[r0:r0 + HALO] / cnt.astype(F32)

    inv_width = 0.5 / half_window(1).astype(F32)
    pooled = jnp.concatenate(
        [edge_mean(0), win[HALO:tm - HALO] * inv_width, edge_mean(tm - HALO)], axis=0)
    return _dot((pooled - xp).astype(BF16), pw_ref[...]) * ps_ref[...]


def _mix_kernel(of_ref, ob_ref, bm_ref, xp_ref, hp_ref, hn_ref, x_ref, gn_ref, lng_ref, lnb_ref,
                wst_ref, sb_ref, pw_ref, ps_ref, wo_ref, out_ref, mixed_ref, *,
                seq_len, tiles_per_seq, n_tiles):
    tm = x_ref.shape[0]
    t = pl.program_id(0)

    @pl.when(t == 0)
    def _():
        mixed_ref[...] = jnp.zeros_like(mixed_ref)

    fill = lax.rem(t, 2)
    use = 1 - fill
    s0 = lax.rem(jnp.minimum(t, n_tiles - 1), tiles_per_seq) * tm

    n_groups = 4
    gw = D_MODEL // n_groups

    def project(g):
        cols = slice(g * gw, (g + 1) * gw)
        out_ref[:, cols] = x_ref[:, cols] + _dot(mixed_ref[use], wo_ref[:, cols])

    project(0)
    a_out = _gla_gated(of_ref, ob_ref, bm_ref, gn_ref)
    project(1)
    b_out = _spatial_gating(bm_ref, lng_ref, lnb_ref, wst_ref, sb_ref)
    project(2)
    c_out = _pool_mixer(xp_ref, hp_ref, hn_ref, pw_ref, ps_ref, s0, seq_len)
    project(3)
    mixed_ref[fill] = jnp.concatenate([a_out, b_out, c_out], axis=-1).astype(BF16)


def _mix(o_f, o_b, bm, xp, x, p, layer):
    bsz, seq, _ = x.shape
    tm = TM_MIX
    n = seq // tm
    total = bsz * n
    hb = tm // HALO
    n_halo_blocks = seq // HALO

    def mixed_tile(t):
        u = jnp.minimum(t, total - 1)
        return u // n, lax.rem(u, n)

    def projected_tile(t):
        u = jnp.maximum(t - 1, 0)
        return u // n, lax.rem(u, n)

    def tile(which):
        return lambda t: (which(t)[0], which(t)[1], 0)

    def halo_before(t):
        b, j = mixed_tile(t)
        return b, jnp.maximum(j * hb - 1, 0), 0

    def halo_after(t):
        b, j = mixed_tile(t)
        return b, jnp.minimum((j + 1) * hb, n_halo_blocks - 1), 0

    layer_block = lambda t: (layer, 0, 0)
    return pl.pallas_call(
        functools.partial(_mix_kernel, seq_len=seq, tiles_per_seq=n, n_tiles=total),
        grid=(total + 1,),
        in_specs=[
            pl.BlockSpec((None, tm, GLA_WIDTH), tile(mixed_tile)),
            pl.BlockSpec((None, tm, GLA_WIDTH), tile(mixed_tile)),
            pl.BlockSpec((None, tm, B_WIDTH), tile(mixed_tile)),
            pl.BlockSpec((None, tm, POOL_WIDTH), tile(mixed_tile)),
            pl.BlockSpec((None, HALO, POOL_WIDTH), halo_before),
            pl.BlockSpec((None, HALO, POOL_WIDTH), halo_after),
            pl.BlockSpec((None, tm, D_MODEL), tile(projected_tile)),
            pl.BlockSpec((None, 1, GLA_WIDTH), layer_block),
            pl.BlockSpec((None, 1, SGU_WIDTH), layer_block),
            pl.BlockSpec((None, 1, SGU_WIDTH), layer_block),
            pl.BlockSpec((None, SGU_HEADS * SGU_CHUNK, SGU_CHUNK), layer_block),
            pl.BlockSpec((None, SGU_CHUNK, SGU_WIDTH), layer_block),
            pl.BlockSpec((None, POOL_WIDTH, POOL_WIDTH), layer_block),
            pl.BlockSpec((None, 1, POOL_WIDTH), layer_block),
            pl.BlockSpec((None, D_MODEL, D_MODEL), layer_block),
        ],
        out_specs=pl.BlockSpec((None, tm, D_MODEL), tile(projected_tile)),
        out_shape=jax.ShapeDtypeStruct((bsz, seq, D_MODEL), F32),
        scratch_shapes=[pltpu.VMEM((2, tm, D_MODEL), BF16)],
        name="mix_out",
        compiler_params=pltpu.CompilerParams(
            dimension_semantics=("arbitrary",),
            vmem_limit_bytes=VMEM_LIMIT_MIX),
    )(o_f, o_b, bm, xp, xp, xp, x, p["gla_norm_g"], p["sgu_ln_g"], p["sgu_ln_b"], p["sgu_w"],
      p["sgu_b"], p["pool_w"], p["pool_scale"], p["w_o"])


def _ffn_kernel(x_ref, hp_ref, hn_ref, g_ref, w1_ref, cw_ref, cb_ref, w2_ref, nf_ref, out_ref,
                hid_ref, *, seq_len, final):
    tm = x_ref.shape[0]
    _, prev_ok, next_ok = _halo_valid(tm, seq_len)
    x = x_ref[...]
    x_ext = jnp.concatenate(
        [jnp.where(prev_ok, hp_ref[...], 0.0), x, jnp.where(next_ok, hn_ref[...], 0.0)], axis=0)
    h_ext = _rmsnorm(x_ext, g_ref[...])
    hb_ext = h_ext.astype(BF16)
    hb = h_ext[HALO:HALO + tm].astype(BF16)
    rows_ext = tm + 2 * HALO
    for c0 in range(0, D_FF, FFN_COL_CHUNK):
        cols = slice(c0, c0 + FFN_COL_CHUNK)
        a = _dot(hb_ext, w1_ref[:, cols])
        up = _dot(hb, w1_ref[:, D_FF + c0:D_FF + c0 + FFN_COL_CHUNK])
        conv = (pltpu.roll(a, 1, 0) * cw_ref[0:1, cols] + a * cw_ref[1:2, cols]
                + pltpu.roll(a, rows_ext - 1, 0) * cw_ref[2:3, cols] + cb_ref[:, cols])
        hid_ref[:, cols] = (jax.nn.silu(conv[HALO:HALO + tm]) * up).astype(BF16)
    y = x + _dot(hid_ref[...], w2_ref[...])
    if final:
        y = _rmsnorm(y, nf_ref[...])
    out_ref[...] = y


def _ffn(x, p, layer, final):
    bsz, seq, _ = x.shape
    tm = TM_FFN
    hb = tm // HALO
    n_halo_blocks = seq // HALO
    tile = lambda b, j: (b, j, 0)
    layer_block = lambda b, j: (layer, 0, 0)
    return pl.pallas_call(
        functools.partial(_ffn_kernel, seq_len=seq, final=final),
        grid=(bsz, seq // tm),
        in_specs=[
            pl.BlockSpec((None, tm, D_MODEL), tile),
            pl.BlockSpec((None, HALO, D_MODEL), lambda b, j: (b, jnp.maximum(j * hb - 1, 0), 0)),
            pl.BlockSpec((None, HALO, D_MODEL),
                         lambda b, j: (b, jnp.minimum((j + 1) * hb, n_halo_blocks - 1), 0)),
            pl.BlockSpec((None, 1, D_MODEL), layer_block),
            pl.BlockSpec((None, D_MODEL, 2 * D_FF), layer_block, pipeline_mode=pl.Buffered(1)),
            pl.BlockSpec((None, 3, D_FF), layer_block),
            pl.BlockSpec((None, 1, D_FF), layer_block),
            pl.BlockSpec((None, D_FF, D_MODEL), layer_block, pipeline_mode=pl.Buffered(1)),
            pl.BlockSpec((1, D_MODEL), lambda b, j: (0, 0)),
        ],
        out_specs=pl.BlockSpec((None, tm, D_MODEL), tile),
        out_shape=jax.ShapeDtypeStruct((bsz, seq, D_MODEL), F32),
        scratch_shapes=[pltpu.VMEM((tm, D_FF), BF16)],
        name="conv_glu_ffn",
        compiler_params=pltpu.CompilerParams(
            dimension_semantics=("arbitrary", "arbitrary"),
            vmem_limit_bytes=VMEM_LIMIT_FFN),
    )(x, x, x, p["norm2_g"], p["w_ffn_in"], p["conv_w"], p["conv_b"], p["w_ffn_out"], p["norm_f"])


def _prepare_params(norm1_g, w_in, gla_gate_w2, gla_gate_b, gla_norm_g, sgu_ln_g, sgu_ln_b,
                    sgu_w, sgu_b, pool_w, pool_scale, w_o, norm2_g, w_ffn_in, conv_w, conv_b,
                    w_ffn_out, norm_f):
    depth = w_in.shape[0]
    row = lambda t: t.reshape(depth, 1, t.shape[-1])
    c_qkv = 2 * GLA_KWIDTH + GLA_WIDTH
    c_g = c_qkv + GLA_WIDTH
    c_lr = c_g + 2 * GLA_GATE_RANK
    w_in_b = w_in.astype(BF16)
    w_in_p = jnp.concatenate([
        w_in_b[:, :, :c_qkv], w_in_b[:, :, c_g:c_lr],
        jnp.zeros((depth, D_MODEL, LR_PAD - 2 * GLA_GATE_RANK), BF16),
        w_in_b[:, :, c_qkv:c_g], w_in_b[:, :, c_lr:]], axis=-1)
    w2cat = jnp.zeros((depth, LR_PAD, 2 * GLA_KWIDTH), F32)
    for d in range(2):
        w2cat = w2cat.at[:, d * GLA_GATE_RANK:(d + 1) * GLA_GATE_RANK,
                         d * GLA_KWIDTH:(d + 1) * GLA_KWIDTH].set(gla_gate_w2[:, d])
    pool_bd = jnp.einsum("lgcd,gh->lgchd", pool_w, jnp.eye(len(POOL_WINDOWS), dtype=pool_w.dtype))
    return {
        "norm1_g": row(norm1_g),
        "w_in": w_in_p,
        "gate_w2": w2cat.astype(BF16),
        "gate_b": gla_gate_b.reshape(depth, 1, 2 * GLA_KWIDTH),
        "gla_norm_g": row(gla_norm_g),
        "sgu_ln_g": row(sgu_ln_g),
        "sgu_ln_b": row(sgu_ln_b),
        "sgu_w": sgu_w.reshape(depth, SGU_HEADS * SGU_CHUNK, SGU_CHUNK).astype(BF16),
        "sgu_b": jnp.repeat(jnp.swapaxes(sgu_b, 1, 2), SGU_HEAD_DIM, axis=2),
        "pool_w": pool_bd.reshape(depth, POOL_WIDTH, POOL_WIDTH).astype(BF16),
        "pool_scale": row(pool_scale),
        "w_o": w_o.astype(BF16),
        "norm2_g": row(norm2_g),
        "w_ffn_in": w_ffn_in.astype(BF16),
        "conv_w": conv_w,
        "conv_b": row(conv_b),
        "w_ffn_out": w_ffn_out.astype(BF16),
        "norm_f": norm_f.reshape(1, D_MODEL),
    }


def _trunk(x, p):
    depth = p["w_in"].shape[0]
    for layer in range(depth):
        qks_f, qks_b, v, dec, bm, xp = _inproj(x, p, layer)
        o_f, o_b = _gla(qks_f, qks_b, v, dec, layer)
        x = _mix(o_f, o_b, bm, xp, x, p, layer)
        x = _ffn(x, p, layer, final=(layer == depth - 1))
    return x


def kernel(x_prompt, x_sample, norm1_g, w_in, gla_gate_w2, gla_gate_b, gla_norm_g, sgu_ln_g, sgu_ln_b, sgu_w, sgu_b, pool_w, pool_scale, w_o, norm2_g, w_ffn_in, conv_w, conv_b, w_ffn_out, norm_f):
    p = _prepare_params(norm1_g, w_in, gla_gate_w2, gla_gate_b, gla_norm_g, sgu_ln_g, sgu_ln_b,
                        sgu_w, sgu_b, pool_w, pool_scale, w_o, norm2_g, w_ffn_in, conv_w, conv_b,
                        w_ffn_out, norm_f)
    return (_trunk(x_prompt, p), _trunk(x_sample, p))
```

```python
import functools

import jax
import jax.numpy as jnp
from jax import lax
from jax.experimental import pallas as pl
from jax.experimental.pallas import tpu as pltpu

F32 = jnp.float32
BF16 = jnp.bfloat16

D_MODEL = 1024
GLA_HEADS = 4
GLA_WIDTH = 512
GLA_DV = 128
GLA_DK = 64
GLA_KWIDTH = 256
GLA_GATE_RANK = 16
GLA_GATE_TAU = 16.0
GLA_CHUNK = 64
SGU_HEADS = 4
SGU_WIDTH = 256
SGU_HEAD_DIM = 64
SGU_CHUNK = 128
POOL_WINDOWS = (2, 4, 8, 16)
POOL_WIDTH = 256
POOL_GROUP_DIM = 64
POOL_HALO = max(POOL_WINDOWS) // 2
D_FF = 2816
EPS = 1e-6

V7X_LANES = 128
V7X_SUBLANES = 8
V7X_VMEM_BYTES = 64 * 1024 * 1024

LR_PAD = V7X_LANES
A_WIDTH = 2 * GLA_KWIDTH + GLA_WIDTH + LR_PAD
B_WIDTH = GLA_WIDTH + 2 * SGU_WIDTH
A_Q, A_K, A_V, A_LR = 0, GLA_KWIDTH, 2 * GLA_KWIDTH, 2 * GLA_KWIDTH + GLA_WIDTH
B_G, B_U, B_VS = 0, GLA_WIDTH, GLA_WIDTH + SGU_WIDTH
IN_WIDTH_PADDED = A_WIDTH + B_WIDTH + POOL_WIDTH

HALO = V7X_SUBLANES
assert POOL_HALO <= HALO

TM_PROJ = 1024
TB_GLA = 2048
TM_MIX = 1024
TM_FFN = 1024
FFN_COL_CHUNK = 256
PROJ_ROW_BLOCK = 256

VMEM_LIMIT_PROJ = 48 * 1024 * 1024
VMEM_LIMIT_GLA = 56 * 1024 * 1024
VMEM_LIMIT_MIX = 48 * 1024 * 1024
VMEM_LIMIT_FFN = 56 * 1024 * 1024
assert max(VMEM_LIMIT_PROJ, VMEM_LIMIT_GLA, VMEM_LIMIT_MIX, VMEM_LIMIT_FFN) < V7X_VMEM_BYTES


def _rmsnorm(x, g):
    return x * lax.rsqrt(jnp.mean(x * x, axis=-1, keepdims=True) + EPS) * g


def _dot(a, b):
    return jnp.dot(a, b, preferred_element_type=F32)


def _dot_nt(a, b):
    return lax.dot_general(a, b, (((1,), (1,)), ((), ())), preferred_element_type=F32)


def _dot_tn(a, b):
    return lax.dot_general(a, b, (((0,), (0,)), ((), ())), preferred_element_type=F32)


def _log_sigmoid(z):
    return jnp.minimum(z, 0.0) - jnp.log(1.0 + jnp.exp(-jnp.abs(z)))


def _chunk_prefix_sums(x):
    row_in_chunk = lax.broadcasted_iota(jnp.int32, x.shape, 0) % GLA_CHUNK
    shift = 1
    while shift < GLA_CHUNK:
        x = x + jnp.where(row_in_chunk >= shift, pltpu.roll(x, shift, 0), 0.0)
        shift *= 2
    return x


def _gate_stages(a_parts, w2_ref, b2_ref, qks_f_ref, qks_b_ref, dec_ref, r0, rb):
    rows = pl.ds(r0, rb)
    n_chunks = rb // GLA_CHUNK
    fwd, bwd = slice(0, GLA_KWIDTH), slice(GLA_KWIDTH, 2 * GLA_KWIDTH)
    val = {}

    def gates():
        z = _dot(a_parts["lr"].astype(BF16), w2_ref[...]) + b2_ref[...]
        val["log_a"] = _log_sigmoid(z) * (1.0 / GLA_GATE_TAU)

    def scan():
        val["prefix"] = _chunk_prefix_sums(val["log_a"])

    def totals():
        prefix = val["prefix"]
        last = [prefix[(c + 1) * GLA_CHUNK - 1:(c + 1) * GLA_CHUNK, :] for c in range(n_chunks)]
        val["total"] = jnp.concatenate(
            [jnp.broadcast_to(t, (GLA_CHUNK, 2 * GLA_KWIDTH)) for t in last], axis=0)
        dec_ref[pl.ds(r0 // GLA_CHUNK * HALO, n_chunks * HALO)] = jnp.concatenate(
            [jnp.broadcast_to(jnp.exp(t), (HALO, 2 * GLA_KWIDTH)) for t in last], axis=0)

    def emit(out_ref, g_cum, g_total):
        if "q" not in val:
            val["q"] = a_parts["q"] * (GLA_DK ** -0.5)
        q, k = val["q"], a_parts["k"]
        out_ref[rows, 0:GLA_KWIDTH] = (q * jnp.exp(g_cum)).astype(BF16)
        out_ref[rows, GLA_KWIDTH:2 * GLA_KWIDTH] = (k * jnp.exp(-g_cum)).astype(BF16)
        out_ref[rows, 2 * GLA_KWIDTH:] = (k * jnp.exp(g_total - g_cum)).astype(BF16)

    def forward():
        emit(qks_f_ref, val["prefix"][:, fwd], val["total"][:, fwd])

    def backward():
        suffix = val["total"][:, bwd] - val["prefix"][:, bwd] + val["log_a"][:, bwd]
        emit(qks_b_ref, suffix, val["total"][:, bwd])

    return [gates, scan, totals, forward, backward]


def _inproj_kernel(x_ref, g_ref, w_ref, w2_ref, b2_ref, qks_f_ref, qks_b_ref, v_ref, dec_ref, b_ref,
                   xp_ref):
    tm = x_ref.shape[0]
    rb = PROJ_ROW_BLOCK
    n_blocks = tm // rb
    hb = [None] * n_blocks
    a_parts = [dict() for _ in range(n_blocks)]

    def normalize(r):
        hb[r] = _rmsnorm(x_ref[pl.ds(r * rb, rb)], g_ref[...]).astype(BF16)

    def project(r, c0, c1):
        return _dot(hb[r], w_ref[:, c0:c1])

    def gla_pieces(r):
        rows = pl.ds(r * rb, rb)

        def part(name, c0, c1):
            return lambda: a_parts[r].__setitem__(name, project(r, c0, c1))

        def values(c0, c1):
            def piece():
                v_ref[rows, c0 - A_V:c1 - A_V] = project(r, c0, c1).astype(BF16)
            return piece

        half_v = A_V + GLA_WIDTH // 2
        return [part("lr", A_LR, A_LR + LR_PAD), part("q", A_Q, A_K), part("k", A_K, A_V),
                values(A_V, half_v), values(half_v, A_LR)]

    def other_pieces(r):
        rows = pl.ds(r * rb, rb)

        def branch(c0, c1):
            def piece():
                b_ref[rows, c0:c1] = project(r, A_WIDTH + c0, A_WIDTH + c1).astype(BF16)
            return piece

        def pool():
            xp_ref[rows] = project(r, A_WIDTH + B_WIDTH, IN_WIDTH_PADDED)

        step = 2 * V7X_LANES
        return [branch(c, c + step) for c in range(0, B_WIDTH, step)] + [pool]

    normalize(0)
    for piece in gla_pieces(0):
        piece()
    for r in range(1, n_blocks):
        normalize(r)
    queue = []
    for r in range(n_blocks):
        if r + 1 < n_blocks:
            queue = gla_pieces(r + 1) + queue
        queue = queue + other_pieces(r)
        for stage in _gate_stages(a_parts[r], w2_ref, b2_ref, qks_f_ref, qks_b_ref, dec_ref, r * rb, rb):
            stage()
            for _ in range(2):
                if queue:
                    queue.pop(0)()
    for piece in queue:
        piece()


def _inproj(x, p, layer):
    bsz, seq, _ = x.shape
    tm = TM_PROJ
    tile = lambda b, j: (b, j, 0)
    layer_block = lambda b, j: (layer, 0, 0)
    return pl.pallas_call(
        _inproj_kernel,
        grid=(bsz, seq // tm),
        in_specs=[
            pl.BlockSpec((None, tm, D_MODEL), tile),
            pl.BlockSpec((None, 1, D_MODEL), layer_block),
            pl.BlockSpec((None, D_MODEL, IN_WIDTH_PADDED), layer_block),
            pl.BlockSpec((None, LR_PAD, 2 * GLA_KWIDTH), layer_block),
            pl.BlockSpec((None, 1, 2 * GLA_KWIDTH), layer_block),
        ],
        out_specs=[
            pl.BlockSpec((None, tm, 3 * GLA_KWIDTH), tile),
            pl.BlockSpec((None, tm, 3 * GLA_KWIDTH), tile),
            pl.BlockSpec((None, tm, GLA_WIDTH), tile),
            pl.BlockSpec((None, tm // GLA_CHUNK * HALO, 2 * GLA_KWIDTH), tile),
            pl.BlockSpec((None, tm, B_WIDTH), tile),
            pl.BlockSpec((None, tm, POOL_WIDTH), tile),
        ],
        out_shape=[
            jax.ShapeDtypeStruct((bsz, seq, 3 * GLA_KWIDTH), BF16),
            jax.ShapeDtypeStruct((bsz, seq, 3 * GLA_KWIDTH), BF16),
            jax.ShapeDtypeStruct((bsz, seq, GLA_WIDTH), BF16),
            jax.ShapeDtypeStruct((bsz, seq // GLA_CHUNK * HALO, 2 * GLA_KWIDTH), F32),
            jax.ShapeDtypeStruct((bsz, seq, B_WIDTH), BF16),
            jax.ShapeDtypeStruct((bsz, seq, POOL_WIDTH), F32),
        ],
        name="inproj",
        compiler_params=pltpu.CompilerParams(
            dimension_semantics=("arbitrary", "arbitrary"),
            vmem_limit_bytes=VMEM_LIMIT_PROJ),
    )(x, p["norm1_g"], p["w_in"], p["gate_w2"], p["gate_b"])


def _gla_chunk_terms(qks_ref, v_ref, dec_ref, lanes, c, reverse):
    rows = pl.ds(c * GLA_CHUNK, GLA_CHUNK)
    q_c = qks_ref[rows, 0:GLA_KWIDTH]
    k_c = qks_ref[rows, GLA_KWIDTH:2 * GLA_KWIDTH]
    ks_c = qks_ref[rows, 2 * GLA_KWIDTH:3 * GLA_KWIDTH]
    v_c = v_ref[rows]
    zero_b = jnp.zeros((), BF16)
    k_head = lax.broadcasted_iota(jnp.int32, (GLA_CHUNK, GLA_KWIDTH), 1) // GLA_DK
    v_head = lax.broadcasted_iota(jnp.int32, (GLA_CHUNK, GLA_WIDTH), 1) // GLA_DV
    k_bd = jnp.concatenate([jnp.where(k_head == h, k_c, zero_b) for h in range(GLA_HEADS)], axis=0)
    v_bd = jnp.concatenate([jnp.where(v_head == h, v_c, zero_b) for h in range(GLA_HEADS)], axis=0)
    i = lax.broadcasted_iota(jnp.int32, (GLA_CHUNK, GLA_HEADS * GLA_CHUNK), 0)
    j = lax.broadcasted_iota(jnp.int32, (GLA_CHUNK, GLA_HEADS * GLA_CHUNK), 1) % GLA_CHUNK
    keep = (j >= i) if reverse else (j <= i)
    att = jnp.where(keep, _dot_nt(q_c, k_bd), 0.0).astype(BF16)
    kv = []
    for pair in range(GLA_HEADS // 2):
        both = _dot_tn(ks_c[:, pair * 2 * GLA_DK:(pair + 1) * 2 * GLA_DK],
                       v_c[:, pair * 2 * GLA_DV:(pair + 1) * 2 * GLA_DV])
        kv.append(both[:GLA_DK, :GLA_DV])
        kv.append(both[GLA_DK:, GLA_DV:])
    dec_row = dec_ref[c * HALO:c * HALO + 1, lanes]
    dec = jnp.transpose(jnp.broadcast_to(dec_row, (GLA_DV, GLA_KWIDTH)))
    return {"q": q_c, "att": att, "v_bd": v_bd, "kv": kv, "dec": dec}


def _gla_chunk_step(terms, state, o_ref, c):
    zero_blk = jnp.zeros((GLA_DK, GLA_DV), BF16)
    rows = slice(c * GLA_CHUNK, (c + 1) * GLA_CHUNK)
    for pair in range(GLA_HEADS // 2):
        h0, h1 = 2 * pair, 2 * pair + 1
        j_lanes = slice(pair * 2 * GLA_CHUNK, (pair + 1) * 2 * GLA_CHUNK)
        k_lanes = slice(pair * 2 * GLA_DK, (pair + 1) * 2 * GLA_DK)
        v_lanes = slice(pair * 2 * GLA_DV, (pair + 1) * 2 * GLA_DV)
        s_bd = jnp.concatenate(
            [jnp.concatenate([state[h0].astype(BF16), zero_blk], axis=1),
             jnp.concatenate([zero_blk, state[h1].astype(BF16)], axis=1)], axis=0)
        lhs = jnp.concatenate([terms["att"][:, j_lanes], terms["q"][:, k_lanes]], axis=1)
        rhs = jnp.concatenate([terms["v_bd"][j_lanes, v_lanes], s_bd], axis=0)
        o_ref[rows, v_lanes] = _dot(lhs, rhs).astype(o_ref.dtype)
    dec = terms["dec"]
    return [dec[h * GLA_DK:(h + 1) * GLA_DK] * state[h] + terms["kv"][h] for h in range(GLA_HEADS)]


def _gla_kernel(qks_f, v_f, dec_f, qks_b, v_b, dec_b, of_ref, ob_ref, st_f, st_b):
    @pl.when(pl.program_id(1) == 0)
    def _():
        st_f[...] = jnp.zeros_like(st_f)
        st_b[...] = jnp.zeros_like(st_b)

    n_chunks = qks_f.shape[0] // GLA_CHUNK
    dirs = (
        dict(qks=qks_f, v=v_f, dec=dec_f, lanes=slice(0, GLA_KWIDTH), o=of_ref, st=st_f, reverse=False),
        dict(qks=qks_b, v=v_b, dec=dec_b, lanes=slice(GLA_KWIDTH, 2 * GLA_KWIDTH), o=ob_ref, st=st_b,
             reverse=True),
    )
    terms = [[_gla_chunk_terms(d["qks"], d["v"], d["dec"], d["lanes"], c, d["reverse"])
              for c in range(n_chunks)] for d in dirs]
    states = [[d["st"][h] for h in range(GLA_HEADS)] for d in dirs]
    for step in range(n_chunks):
        for n, d in enumerate(dirs):
            c = n_chunks - 1 - step if d["reverse"] else step
            states[n] = _gla_chunk_step(terms[n][c], states[n], d["o"], c)
    for n, d in enumerate(dirs):
        for h in range(GLA_HEADS):
            d["st"][h] = states[n][h]


def _gla(qks_f, qks_b, v, dec, layer):
    bsz, seq, _ = v.shape
    tb = TB_GLA
    n = seq // tb
    dec_rows = tb // GLA_CHUNK * HALO
    fwd = lambda b, i: (b, i, 0)
    bwd = lambda b, i: (b, n - 1 - i, 0)
    return pl.pallas_call(
        _gla_kernel,
        grid=(bsz, n),
        in_specs=[
            pl.BlockSpec((None, tb, 3 * GLA_KWIDTH), fwd),
            pl.BlockSpec((None, tb, GLA_WIDTH), fwd),
            pl.BlockSpec((None, dec_rows, 2 * GLA_KWIDTH), fwd),
            pl.BlockSpec((None, tb, 3 * GLA_KWIDTH), bwd),
            pl.BlockSpec((None, tb, GLA_WIDTH), bwd),
            pl.BlockSpec((None, dec_rows, 2 * GLA_KWIDTH), bwd),
        ],
        out_specs=[
            pl.BlockSpec((None, tb, GLA_WIDTH), fwd),
            pl.BlockSpec((None, tb, GLA_WIDTH), bwd),
        ],
        out_shape=[jax.ShapeDtypeStruct((bsz, seq, GLA_WIDTH), BF16)] * 2,
        scratch_shapes=[pltpu.VMEM((GLA_HEADS, GLA_DK, GLA_DV), F32)] * 2,
        name="gla_scan",
        compiler_params=pltpu.CompilerParams(
            dimension_semantics=("arbitrary", "arbitrary"),
            vmem_limit_bytes=VMEM_LIMIT_GLA),
    )(qks_f, v, dec, qks_b, v, dec)


def _halo_valid(tm, seq_len):
    s0 = pl.program_id(1) * tm
    return s0, s0 > 0, s0 + tm < seq_len


def _gla_gated(of_ref, ob_ref, bm_ref, gn_ref):
    o = of_ref[...].astype(F32) + ob_ref[...].astype(F32)
    heads = []
    for h in range(GLA_HEADS):
        o_h = o[:, h * GLA_DV:(h + 1) * GLA_DV]
        heads.append(o_h * lax.rsqrt(jnp.mean(o_h * o_h, axis=-1, keepdims=True) + EPS))
    gate = bm_ref[:, B_G:B_G + GLA_WIDTH].astype(F32)
    return jnp.concatenate(heads, axis=-1) * gn_ref[...] * jax.nn.silu(gate)


def _spatial_gating(bm_ref, lng_ref, lnb_ref, wst_ref, sb_ref):
    tm = bm_ref.shape[0]
    gu = jax.nn.gelu(bm_ref[:, B_U:B_U + SGU_WIDTH].astype(F32))
    gv = jax.nn.gelu(bm_ref[:, B_VS:B_VS + SGU_WIDTH].astype(F32))
    xc = gv - jnp.mean(gv, axis=-1, keepdims=True)
    v_ln = xc * lax.rsqrt(jnp.mean(xc * xc, axis=-1, keepdims=True) + EPS) * lng_ref[...] + lnb_ref[...]
    v_b = v_ln.astype(BF16)
    head_of_lane = lax.broadcasted_iota(jnp.int32, (SGU_CHUNK, SGU_WIDTH), 1) // SGU_HEAD_DIM
    parts = []
    for n in range(tm // SGU_CHUNK):
        rows = slice(n * SGU_CHUNK, (n + 1) * SGU_CHUNK)
        all_heads = _dot(wst_ref[...], v_b[rows])
        mixed = jnp.zeros((SGU_CHUNK, SGU_WIDTH), F32)
        for h in range(SGU_HEADS):
            mixed = mixed + jnp.where(head_of_lane == h, all_heads[h * SGU_CHUNK:(h + 1) * SGU_CHUNK], 0.0)
        parts.append(gu[rows] * (mixed + sb_ref[...]))
    return jnp.concatenate(parts, axis=0)


def _pool_mixer(xp_ref, hp_ref, hn_ref, pw_ref, ps_ref, s0, seq_len):
    tm = xp_ref.shape[0]
    xp = xp_ref[...]
    ext = jnp.concatenate(
        [jnp.where(s0 > 0, hp_ref[...], 0.0), xp, jnp.where(s0 + tm < seq_len, hn_ref[...], 0.0)], axis=0)
    rows_ext = tm + 2 * HALO
    back = lambda t, d: pltpu.roll(t, d, 0)
    ahead = lambda t, d: pltpu.roll(t, rows_ext - d, 0)
    core = lambda t: t[HALO:HALO + tm]
    lo, hi = ext[:, :V7X_LANES], ext[:, V7X_LANES:]
    lo2, hi2 = lo + back(lo, 1), hi + back(hi, 1)
    lo4, hi4 = back(lo2, 1) + ahead(lo2, 1), back(hi2, 1) + ahead(hi2, 1)
    hi8 = back(hi4, 2) + ahead(hi4, 2)
    hi16 = back(hi8, 4) + ahead(hi8, 4)
    first_group = lax.broadcasted_iota(jnp.int32, (tm, V7X_LANES), 1) < POOL_GROUP_DIM
    win = jnp.concatenate([jnp.where(first_group, core(lo2), core(lo4)),
                           jnp.where(first_group, core(hi8), core(hi16))], axis=1)

    def half_window(rows):
        group = lax.broadcasted_iota(jnp.int32, (rows, POOL_WIDTH), 1) // POOL_GROUP_DIM
        return jnp.where(group == 0, 1, jnp.where(group == 1, 2, jnp.where(group == 2, 4, 8)))

    def edge_mean(r0):
        half = half_window(HALO)
        t = s0 + r0 + lax.broadcasted_iota(jnp.int32, (HALO, POOL_WIDTH), 0)
        cnt = jnp.minimum(t + half, seq_len) - jnp.maximum(t - half, 0)
        return win[r0:r0 + HALO] / cnt.astype(F32)

    inv_width = 0.5 / half_window(1).astype(F32)
    pooled = jnp.concatenate(
        [edge_mean(0), win[HALO:tm - HALO] * inv_width, edge_mean(tm - HALO)], axis=0)
    return _dot((pooled - xp).astype(BF16), pw_ref[...]) * ps_ref[...]


def _mix_kernel(of_ref, ob_ref, bm_ref, xp_ref, hp_ref, hn_ref, x_ref, gn_ref, lng_ref, lnb_ref,
                wst_ref, sb_ref, pw_ref, ps_ref, wo_ref, out_ref, mixed_ref, *,
                seq_len, tiles_per_seq, n_tiles):
    tm = x_ref.shape[0]
    t = pl.program_id(0)

    @pl.when(t == 0)
    def _():
        mixed_ref[...] = jnp.zeros_like(mixed_ref)

    fill = lax.rem(t, 2)
    use = 1 - fill
    s0 = lax.rem(jnp.minimum(t, n_tiles - 1), tiles_per_seq) * tm

    n_groups = 4
    gw = D_MODEL // n_groups

    def project(g):
        cols = slice(g * gw, (g + 1) * gw)
        out_ref[:, cols] = x_ref[:, cols] + _dot(mixed_ref[use], wo_ref[:, cols])

    project(0)
    a_out = _gla_gated(of_ref, ob_ref, bm_ref, gn_ref)
    project(1)
    b_out = _spatial_gating(bm_ref, lng_ref, lnb_ref, wst_ref, sb_ref)
    project(2)
    c_out = _pool_mixer(xp_ref, hp_ref, hn_ref, pw_ref, ps_ref, s0, seq_len)
    project(3)
    mixed_ref[fill] = jnp.concatenate([a_out, b_out, c_out], axis=-1).astype(BF16)


def _mix(o_f, o_b, bm, xp, x, p, layer):
    bsz, seq, _ = x.shape
    tm = TM_MIX
    n = seq // tm
    total = bsz * n
    hb = tm // HALO
    n_halo_blocks = seq // HALO

    def mixed_tile(t):
        u = jnp.minimum(t, total - 1)
        return u // n, lax.rem(u, n)

    def projected_tile(t):
        u = jnp.maximum(t - 1, 0)
        return u // n, lax.rem(u, n)

    def tile(which):
        return lambda t: (which(t)[0], which(t)[1], 0)

    def halo_before(t):
        b, j = mixed_tile(t)
        return b, jnp.maximum(j * hb - 1, 0), 0

    def halo_after(t):
        b, j = mixed_tile(t)
        return b, jnp.minimum((j + 1) * hb, n_halo_blocks - 1), 0

    layer_block = lambda t: (layer, 0, 0)
    return pl.pallas_call(
        functools.partial(_mix_kernel, seq_len=seq, tiles_per_seq=n, n_tiles=total),
        grid=(total + 1,),
        in_specs=[
            pl.BlockSpec((None, tm, GLA_WIDTH), tile(mixed_tile)),
            pl.BlockSpec((None, tm, GLA_WIDTH), tile(mixed_tile)),
            pl.BlockSpec((None, tm, B_WIDTH), tile(mixed_tile)),
            pl.BlockSpec((None, tm, POOL_WIDTH), tile(mixed_tile)),
            pl.BlockSpec((None, HALO, POOL_WIDTH), halo_before),
            pl.BlockSpec((None, HALO, POOL_WIDTH), halo_after),
            pl.BlockSpec((None, tm, D_MODEL), tile(projected_tile)),
            pl.BlockSpec((None, 1, GLA_WIDTH), layer_block),
            pl.BlockSpec((None, 1, SGU_WIDTH), layer_block),
            pl.BlockSpec((None, 1, SGU_WIDTH), layer_block),
            pl.BlockSpec((None, SGU_HEADS * SGU_CHUNK, SGU_CHUNK), layer_block),
            pl.BlockSpec((None, SGU_CHUNK, SGU_WIDTH), layer_block),
            pl.BlockSpec((None, POOL_WIDTH, POOL_WIDTH), layer_block),
            pl.BlockSpec((None, 1, POOL_WIDTH), layer_block),
            pl.BlockSpec((None, D_MODEL, D_MODEL), layer_block),
        ],
        out_specs=pl.BlockSpec((None, tm, D_MODEL), tile(projected_tile)),
        out_shape=jax.ShapeDtypeStruct((bsz, seq, D_MODEL), F32),
        scratch_shapes=[pltpu.VMEM((2, tm, D_MODEL), BF16)],
        name="mix_out",
        compiler_params=pltpu.CompilerParams(
            dimension_semantics=("arbitrary",),
            vmem_limit_bytes=VMEM_LIMIT_MIX),
    )(o_f, o_b, bm, xp, xp, xp, x, p["gla_norm_g"], p["sgu_ln_g"], p["sgu_ln_b"], p["sgu_w"],
      p["sgu_b"], p["pool_w"], p["pool_scale"], p["w_o"])


def _ffn_kernel(x_ref, hp_ref, hn_ref, g_ref, w1_ref, cw_ref, cb_ref, w2_ref, nf_ref, out_ref,
                hid_ref, *, seq_len, final):
    tm = x_ref.shape[0]
    _, prev_ok, next_ok = _halo_valid(tm, seq_len)
    x = x_ref[...]
    x_ext = jnp.concatenate(
        [jnp.where(prev_ok, hp_ref[...], 0.0), x, jnp.where(next_ok, hn_ref[...], 0.0)], axis=0)
    h_ext = _rmsnorm(x_ext, g_ref[...])
    hb_ext = h_ext.astype(BF16)
    hb = h_ext[HALO:HALO + tm].astype(BF16)
    rows_ext = tm + 2 * HALO
    for c0 in range(0, D_FF, FFN_COL_CHUNK):
        cols = slice(c0, c0 + FFN_COL_CHUNK)
        a = _dot(hb_ext, w1_ref[:, cols])
        up = _dot(hb, w1_ref[:, D_FF + c0:D_FF + c0 + FFN_COL_CHUNK])
        conv = (pltpu.roll(a, 1, 0) * cw_ref[0:1, cols] + a * cw_ref[1:2, cols]
                + pltpu.roll(a, rows_ext - 1, 0) * cw_ref[2:3, cols] + cb_ref[:, cols])
        hid_ref[:, cols] = (jax.nn.silu(conv[HALO:HALO + tm]) * up).astype(BF16)
    y = x + _dot(hid_ref[...], w2_ref[...])
    if final:
        y = _rmsnorm(y, nf_ref[...])
    out_ref[...] = y


def _ffn(x, p, layer, final):
    bsz, seq, _ = x.shape
    tm = TM_FFN
    hb = tm // HALO
    n_halo_blocks = seq // HALO
    tile = lambda b, j: (b, j, 0)
    layer_block = lambda b, j: (layer, 0, 0)
    return pl.pallas_call(
        functools.partial(_ffn_kernel, seq_len=seq, final=final),
        grid=(bsz, seq // tm),
        in_specs=[
            pl.BlockSpec((None, tm, D_MODEL), tile),
            pl.BlockSpec((None, HALO, D_MODEL), lambda b, j: (b, jnp.maximum(j * hb - 1, 0), 0)),
            pl.BlockSpec((None, HALO, D_MODEL),
                         lambda b, j: (b, jnp.minimum((j + 1) * hb, n_halo_blocks - 1), 0)),
            pl.BlockSpec((None, 1, D_MODEL), layer_block),
            pl.BlockSpec((None, D_MODEL, 2 * D_FF), layer_block, pipeline_mode=pl.Buffered(1)),
            pl.BlockSpec((None, 3, D_FF), layer_block),
            pl.BlockSpec((None, 1, D_FF), layer_block),
            pl.BlockSpec((None, D_FF, D_MODEL), layer_block, pipeline_mode=pl.Buffered(1)),
            pl.BlockSpec((1, D_MODEL), lambda b, j: (0, 0)),
        ],
        out_specs=pl.BlockSpec((None, tm, D_MODEL), tile),
        out_shape=jax.ShapeDtypeStruct((bsz, seq, D_MODEL), F32),
        scratch_shapes=[pltpu.VMEM((tm, D_FF), BF16)],
        name="conv_glu_ffn",
        compiler_params=pltpu.CompilerParams(
            dimension_semantics=("arbitrary", "arbitrary"),
            vmem_limit_bytes=VMEM_LIMIT_FFN),
    )(x, x, x, p["norm2_g"], p["w_ffn_in"], p["conv_w"], p["conv_b"], p["w_ffn_out"], p["norm_f"])


def _prepare_params(norm1_g, w_in, gla_gate_w2, gla_gate_b, gla_norm_g, sgu_ln_g, sgu_ln_b,
                    sgu_w, sgu_b, pool_w, pool_scale, w_o, norm2_g, w_ffn_in, conv_w, conv_b,
                    w_ffn_out, norm_f):
    depth = w_in.shape[0]
    row = lambda t: t.reshape(depth, 1, t.shape[-1])
    c_qkv = 2 * GLA_KWIDTH + GLA_WIDTH
    c_g = c_qkv + GLA_WIDTH
    c_lr = c_g + 2 * GLA_GATE_RANK
    w_in_b = w_in.astype(BF16)
    w_in_p = jnp.concatenate([
        w_in_b[:, :, :c_qkv], w_in_b[:, :, c_g:c_lr],
        jnp.zeros((depth, D_MODEL, LR_PAD - 2 * GLA_GATE_RANK), BF16),
        w_in_b[:, :, c_qkv:c_g], w_in_b[:, :, c_lr:]], axis=-1)
    w2cat = jnp.zeros((depth, LR_PAD, 2 * GLA_KWIDTH), F32)
    for d in range(2):
        w2cat = w2cat.at[:, d * GLA_GATE_RANK:(d + 1) * GLA_GATE_RANK,
                         d * GLA_KWIDTH:(d + 1) * GLA_KWIDTH].set(gla_gate_w2[:, d])
    pool_bd = jnp.einsum("lgcd,gh->lgchd", pool_w, jnp.eye(len(POOL_WINDOWS), dtype=pool_w.dtype))
    return {
        "norm1_g": row(norm1_g),
        "w_in": w_in_p,
        "gate_w2": w2cat.astype(BF16),
        "gate_b": gla_gate_b.reshape(depth, 1, 2 * GLA_KWIDTH),
        "gla_norm_g": row(gla_norm_g),
        "sgu_ln_g": row(sgu_ln_g),
        "sgu_ln_b": row(sgu_ln_b),
        "sgu_w": sgu_w.reshape(depth, SGU_HEADS * SGU_CHUNK, SGU_CHUNK).astype(BF16),
        "sgu_b": jnp.repeat(jnp.swapaxes(sgu_b, 1, 2), SGU_HEAD_DIM, axis=2),
        "pool_w": pool_bd.reshape(depth, POOL_WIDTH, POOL_WIDTH).astype(BF16),
        "pool_scale": row(pool_scale),
        "w_o": w_o.astype(BF16),
        "norm2_g": row(norm2_g),
        "w_ffn_in": w_ffn_in.astype(BF16),
        "conv_w": conv_w,
        "conv_b": row(conv_b),
        "w_ffn_out": w_ffn_out.astype(BF16),
        "norm_f": norm_f.reshape(1, D_MODEL),
    }


def _trunk(x, p):
    depth = p["w_in"].shape[0]
    for layer in range(depth):
        qks_f, qks_b, v, dec, bm, xp = _inproj(x, p, layer)
        o_f, o_b = _gla(qks_f, qks_b, v, dec, layer)
        x = _mix(o_f, o_b, bm, xp, x, p, layer)
        x = _ffn(x, p, layer, final=(layer == depth - 1))
    return x


def kernel(x_prompt, x_sample, norm1_g, w_in, gla_gate_w2, gla_gate_b, gla_norm_g, sgu_ln_g, sgu_ln_b, sgu_w, sgu_b, pool_w, pool_scale, w_o, norm2_g, w_ffn_in, conv_w, conv_b, w_ffn_out, norm_f):
    p = _prepare_params(norm1_g, w_in, gla_gate_w2, gla_gate_b, gla_norm_g, sgu_ln_g, sgu_ln_b,
                        sgu_w, sgu_b, pool_w, pool_scale, w_o, norm2_g, w_ffn_in, conv_w, conv_b,
                        w_ffn_out, norm_f)
    return (_trunk(x_prompt, p), _trunk(x_sample, p))
```

```python
import functools

import jax
import jax.numpy as jnp
from jax import lax
from jax.experimental import pallas as pl
from jax.experimental.pallas import tpu as pltpu

F32 = jnp.float32
BF16 = jnp.bfloat16

D_MODEL = 1024
GLA_HEADS = 4
GLA_WIDTH = 512
GLA_DV = 128
GLA_DK = 64
GLA_KWIDTH = 256
GLA_GATE_RANK = 16
GLA_GATE_TAU = 16.0
GLA_CHUNK = 64
SGU_HEADS = 4
SGU_WIDTH = 256
SGU_HEAD_DIM = 64
SGU_CHUNK = 128
POOL_WINDOWS = (2, 4, 8, 16)
POOL_WIDTH = 256
POOL_GROUP_DIM = 64
POOL_HALO = max(POOL_WINDOWS) // 2
D_FF = 2816
EPS = 1e-6

V7X_LANES = 128
V7X_SUBLANES = 8
V7X_VMEM_BYTES = 64 * 1024 * 1024

LR_PAD = V7X_LANES
A_WIDTH = 2 * GLA_KWIDTH + GLA_WIDTH + LR_PAD
B_WIDTH = GLA_WIDTH + 2 * SGU_WIDTH
A_Q, A_K, A_V, A_LR = 0, GLA_KWIDTH, 2 * GLA_KWIDTH, 2 * GLA_KWIDTH + GLA_WIDTH
B_G, B_U, B_VS = 0, GLA_WIDTH, GLA_WIDTH + SGU_WIDTH
IN_WIDTH_PADDED = A_WIDTH + B_WIDTH + POOL_WIDTH

HALO = V7X_SUBLANES
assert POOL_HALO <= HALO

TM_PROJ = 1024
TB_GLA = 2048
TM_MIX = 1024
TM_FFN = 1024
FFN_COL_CHUNK = 256
PROJ_ROW_BLOCK = 128

VMEM_LIMIT_PROJ = 48 * 1024 * 1024
VMEM_LIMIT_GLA = 56 * 1024 * 1024
VMEM_LIMIT_MIX = 48 * 1024 * 1024
VMEM_LIMIT_FFN = 56 * 1024 * 1024
assert max(VMEM_LIMIT_PROJ, VMEM_LIMIT_GLA, VMEM_LIMIT_MIX, VMEM_LIMIT_FFN) < V7X_VMEM_BYTES


def _rmsnorm(x, g):
    return x * lax.rsqrt(jnp.mean(x * x, axis=-1, keepdims=True) + EPS) * g


def _dot(a, b):
    return jnp.dot(a, b, preferred_element_type=F32)


def _dot_nt(a, b):
    return lax.dot_general(a, b, (((1,), (1,)), ((), ())), preferred_element_type=F32)


def _dot_tn(a, b):
    return lax.dot_general(a, b, (((0,), (0,)), ((), ())), preferred_element_type=F32)


def _log_sigmoid(z):
    return jnp.minimum(z, 0.0) - jnp.log(1.0 + jnp.exp(-jnp.abs(z)))


def _chunk_prefix_sums(x):
    row_in_chunk = lax.broadcasted_iota(jnp.int32, x.shape, 0) % GLA_CHUNK
    shift = 1
    while shift < GLA_CHUNK:
        x = x + jnp.where(row_in_chunk >= shift, pltpu.roll(x, shift, 0), 0.0)
        shift *= 2
    return x


def _gate_stages(a_parts, w2_ref, b2_ref, qks_f_ref, qks_b_ref, dec_ref, r0, rb):
    rows = pl.ds(r0, rb)
    n_chunks = rb // GLA_CHUNK
    fwd, bwd = slice(0, GLA_KWIDTH), slice(GLA_KWIDTH, 2 * GLA_KWIDTH)
    val = {}

    def gates():
        z = _dot(a_parts["lr"].astype(BF16), w2_ref[...]) + b2_ref[...]
        val["log_a"] = _log_sigmoid(z) * (1.0 / GLA_GATE_TAU)

    def scan():
        val["prefix"] = _chunk_prefix_sums(val["log_a"])

    def totals():
        prefix = val["prefix"]
        last = [prefix[(c + 1) * GLA_CHUNK - 1:(c + 1) * GLA_CHUNK, :] for c in range(n_chunks)]
        val["total"] = jnp.concatenate(
            [jnp.broadcast_to(t, (GLA_CHUNK, 2 * GLA_KWIDTH)) for t in last], axis=0)
        dec_ref[pl.ds(r0 // GLA_CHUNK * HALO, n_chunks * HALO)] = jnp.concatenate(
            [jnp.broadcast_to(jnp.exp(t), (HALO, 2 * GLA_KWIDTH)) for t in last], axis=0)

    def emit(out_ref, g_cum, g_total):
        if "q" not in val:
            val["q"] = a_parts["q"] * (GLA_DK ** -0.5)
        q, k = val["q"], a_parts["k"]
        out_ref[rows, 0:GLA_KWIDTH] = (q * jnp.exp(g_cum)).astype(BF16)
        out_ref[rows, GLA_KWIDTH:2 * GLA_KWIDTH] = (k * jnp.exp(-g_cum)).astype(BF16)
        out_ref[rows, 2 * GLA_KWIDTH:] = (k * jnp.exp(g_total - g_cum)).astype(BF16)

    def forward():
        emit(qks_f_ref, val["prefix"][:, fwd], val["total"][:, fwd])

    def backward():
        suffix = val["total"][:, bwd] - val["prefix"][:, bwd] + val["log_a"][:, bwd]
        emit(qks_b_ref, suffix, val["total"][:, bwd])

    return [gates, scan, totals, forward, backward]


def _inproj_kernel(x_ref, g_ref, w_ref, w2_ref, b2_ref, qks_f_ref, qks_b_ref, v_ref, dec_ref, b_ref,
                   xp_ref):
    tm = x_ref.shape[0]
    rb = PROJ_ROW_BLOCK
    n_blocks = tm // rb
    hb = [None] * n_blocks
    a_parts = [dict() for _ in range(n_blocks)]

    def normalize(r):
        hb[r] = _rmsnorm(x_ref[pl.ds(r * rb, rb)], g_ref[...]).astype(BF16)

    def project(r, c0, c1):
        return _dot(hb[r], w_ref[:, c0:c1])

    def gla_pieces(r):
        rows = pl.ds(r * rb, rb)

        def part(name, c0, c1):
            return lambda: a_parts[r].__setitem__(name, project(r, c0, c1))

        def values(c0, c1):
            def piece():
                v_ref[rows, c0 - A_V:c1 - A_V] = project(r, c0, c1).astype(BF16)
            return piece

        half_v = A_V + GLA_WIDTH // 2
        return [part("lr", A_LR, A_LR + LR_PAD), part("q", A_Q, A_K), part("k", A_K, A_V),
                values(A_V, half_v), values(half_v, A_LR)]

    def other_pieces(r):
        rows = pl.ds(r * rb, rb)

        def branch(c0, c1):
            def piece():
                b_ref[rows, c0:c1] = project(r, A_WIDTH + c0, A_WIDTH + c1).astype(BF16)
            return piece

        def pool():
            xp_ref[rows] = project(r, A_WIDTH + B_WIDTH, IN_WIDTH_PADDED)

        step = 2 * V7X_LANES
        return [branch(c, c + step) for c in range(0, B_WIDTH, step)] + [pool]

    normalize(0)
    for piece in gla_pieces(0):
        piece()
    for r in range(1, n_blocks):
        normalize(r)
    queue = []
    for r in range(n_blocks):
        if r + 1 < n_blocks:
            queue = gla_pieces(r + 1) + queue
        queue = queue + other_pieces(r)
        for stage in _gate_stages(a_parts[r], w2_ref, b2_ref, qks_f_ref, qks_b_ref, dec_ref, r * rb, rb):
            stage()
            for _ in range(2):
                if queue:
                    queue.pop(0)()
    for piece in queue:
        piece()


def _inproj(x, p, layer):
    bsz, seq, _ = x.shape
    tm = TM_PROJ
    tile = lambda b, j: (b, j, 0)
    layer_block = lambda b, j: (layer, 0, 0)
    return pl.pallas_call(
        _inproj_kernel,
        grid=(bsz, seq // tm),
        in_specs=[
            pl.BlockSpec((None, tm, D_MODEL), tile),
            pl.BlockSpec((None, 1, D_MODEL), layer_block),
            pl.BlockSpec((None, D_MODEL, IN_WIDTH_PADDED), layer_block),
            pl.BlockSpec((None, LR_PAD, 2 * GLA_KWIDTH), layer_block),
            pl.BlockSpec((None, 1, 2 * GLA_KWIDTH), layer_block),
        ],
        out_specs=[
            pl.BlockSpec((None, tm, 3 * GLA_KWIDTH), tile),
            pl.BlockSpec((None, tm, 3 * GLA_KWIDTH), tile),
            pl.BlockSpec((None, tm, GLA_WIDTH), tile),
            pl.BlockSpec((None, tm // GLA_CHUNK * HALO, 2 * GLA_KWIDTH), tile),
            pl.BlockSpec((None, tm, B_WIDTH), tile),
            pl.BlockSpec((None, tm, POOL_WIDTH), tile),
        ],
        out_shape=[
            jax.ShapeDtypeStruct((bsz, seq, 3 * GLA_KWIDTH), BF16),
            jax.ShapeDtypeStruct((bsz, seq, 3 * GLA_KWIDTH), BF16),
            jax.ShapeDtypeStruct((bsz, seq, GLA_WIDTH), BF16),
            jax.ShapeDtypeStruct((bsz, seq // GLA_CHUNK * HALO, 2 * GLA_KWIDTH), F32),
            jax.ShapeDtypeStruct((bsz, seq, B_WIDTH), BF16),
            jax.ShapeDtypeStruct((bsz, seq, POOL_WIDTH), F32),
        ],
        name="inproj",
        compiler_params=pltpu.CompilerParams(
            dimension_semantics=("arbitrary", "arbitrary"),
            vmem_limit_bytes=VMEM_LIMIT_PROJ),
    )(x, p["norm1_g"], p["w_in"], p["gate_w2"], p["gate_b"])


def _gla_chunk_terms(qks_ref, v_ref, dec_ref, lanes, c, reverse):
    rows = pl.ds(c * GLA_CHUNK, GLA_CHUNK)
    q_c = qks_ref[rows, 0:GLA_KWIDTH]
    k_c = qks_ref[rows, GLA_KWIDTH:2 * GLA_KWIDTH]
    ks_c = qks_ref[rows, 2 * GLA_KWIDTH:3 * GLA_KWIDTH]
    v_c = v_ref[rows]
    zero_b = jnp.zeros((), BF16)
    k_head = lax.broadcasted_iota(jnp.int32, (GLA_CHUNK, GLA_KWIDTH), 1) // GLA_DK
    v_head = lax.broadcasted_iota(jnp.int32, (GLA_CHUNK, GLA_WIDTH), 1) // GLA_DV
    k_bd = jnp.concatenate([jnp.where(k_head == h, k_c, zero_b) for h in range(GLA_HEADS)], axis=0)
    v_bd = jnp.concatenate([jnp.where(v_head == h, v_c, zero_b) for h in range(GLA_HEADS)], axis=0)
    i = lax.broadcasted_iota(jnp.int32, (GLA_CHUNK, GLA_HEADS * GLA_CHUNK), 0)
    j = lax.broadcasted_iota(jnp.int32, (GLA_CHUNK, GLA_HEADS * GLA_CHUNK), 1) % GLA_CHUNK
    keep = (j >= i) if reverse else (j <= i)
    att = jnp.where(keep, _dot_nt(q_c, k_bd), 0.0).astype(BF16)
    kv = []
    for pair in range(GLA_HEADS // 2):
        both = _dot_tn(ks_c[:, pair * 2 * GLA_DK:(pair + 1) * 2 * GLA_DK],
                       v_c[:, pair * 2 * GLA_DV:(pair + 1) * 2 * GLA_DV])
        kv.append(both[:GLA_DK, :GLA_DV])
        kv.append(both[GLA_DK:, GLA_DV:])
    dec_row = dec_ref[c * HALO:c * HALO + 1, lanes]
    dec = jnp.transpose(jnp.broadcast_to(dec_row, (GLA_DV, GLA_KWIDTH)))
    return {"q": q_c, "att": att, "v_bd": v_bd, "kv": kv, "dec": dec}


def _gla_chunk_step(terms, state, o_ref, c):
    zero_blk = jnp.zeros((GLA_DK, GLA_DV), BF16)
    rows = slice(c * GLA_CHUNK, (c + 1) * GLA_CHUNK)
    for pair in range(GLA_HEADS // 2):
        h0, h1 = 2 * pair, 2 * pair + 1
        j_lanes = slice(pair * 2 * GLA_CHUNK, (pair + 1) * 2 * GLA_CHUNK)
        k_lanes = slice(pair * 2 * GLA_DK, (pair + 1) * 2 * GLA_DK)
        v_lanes = slice(pair * 2 * GLA_DV, (pair + 1) * 2 * GLA_DV)
        s_bd = jnp.concatenate(
            [jnp.concatenate([state[h0].astype(BF16), zero_blk], axis=1),
             jnp.concatenate([zero_blk, state[h1].astype(BF16)], axis=1)], axis=0)
        lhs = jnp.concatenate([terms["att"][:, j_lanes], terms["q"][:, k_lanes]], axis=1)
        rhs = jnp.concatenate([terms["v_bd"][j_lanes, v_lanes], s_bd], axis=0)
        o_ref[rows, v_lanes] = _dot(lhs, rhs).astype(o_ref.dtype)
    dec = terms["dec"]
    return [dec[h * GLA_DK:(h + 1) * GLA_DK] * state[h] + terms["kv"][h] for h in range(GLA_HEADS)]


def _gla_kernel(qks_f, v_f, dec_f, qks_b, v_b, dec_b, of_ref, ob_ref, st_f, st_b):
    @pl.when(pl.program_id(1) == 0)
    def _():
        st_f[...] = jnp.zeros_like(st_f)
        st_b[...] = jnp.zeros_like(st_b)

    n_chunks = qks_f.shape[0] // GLA_CHUNK
    dirs = (
        dict(qks=qks_f, v=v_f, dec=dec_f, lanes=slice(0, GLA_KWIDTH), o=of_ref, st=st_f, reverse=False),
        dict(qks=qks_b, v=v_b, dec=dec_b, lanes=slice(GLA_KWIDTH, 2 * GLA_KWIDTH), o=ob_ref, st=st_b,
             reverse=True),
    )
    terms = [[_gla_chunk_terms(d["qks"], d["v"], d["dec"], d["lanes"], c, d["reverse"])
              for c in range(n_chunks)] for d in dirs]
    states = [[d["st"][h] for h in range(GLA_HEADS)] for d in dirs]
    for step in range(n_chunks):
        for n, d in enumerate(dirs):
            c = n_chunks - 1 - step if d["reverse"] else step
            states[n] = _gla_chunk_step(terms[n][c], states[n], d["o"], c)
    for n, d in enumerate(dirs):
        for h in range(GLA_HEADS):
            d["st"][h] = states[n][h]


def _gla(qks_f, qks_b, v, dec, layer):
    bsz, seq, _ = v.shape
    tb = TB_GLA
    n = seq // tb
    dec_rows = tb // GLA_CHUNK * HALO
    fwd = lambda b, i: (b, i, 0)
    bwd = lambda b, i: (b, n - 1 - i, 0)
    return pl.pallas_call(
        _gla_kernel,
        grid=(bsz, n),
        in_specs=[
            pl.BlockSpec((None, tb, 3 * GLA_KWIDTH), fwd),
            pl.BlockSpec((None, tb, GLA_WIDTH), fwd),
            pl.BlockSpec((None, dec_rows, 2 * GLA_KWIDTH), fwd),
            pl.BlockSpec((None, tb, 3 * GLA_KWIDTH), bwd),
            pl.BlockSpec((None, tb, GLA_WIDTH), bwd),
            pl.BlockSpec((None, dec_rows, 2 * GLA_KWIDTH), bwd),
        ],
        out_specs=[
            pl.BlockSpec((None, tb, GLA_WIDTH), fwd),
            pl.BlockSpec((None, tb, GLA_WIDTH), bwd),
        ],
        out_shape=[jax.ShapeDtypeStruct((bsz, seq, GLA_WIDTH), BF16)] * 2,
        scratch_shapes=[pltpu.VMEM((GLA_HEADS, GLA_DK, GLA_DV), F32)] * 2,
        name="gla_scan",
        compiler_params=pltpu.CompilerParams(
            dimension_semantics=("arbitrary", "arbitrary"),
            vmem_limit_bytes=VMEM_LIMIT_GLA),
    )(qks_f, v, dec, qks_b, v, dec)


def _halo_valid(tm, seq_len):
    s0 = pl.program_id(1) * tm
    return s0, s0 > 0, s0 + tm < seq_len


def _gla_gated(of_ref, ob_ref, bm_ref, gn_ref):
    o = of_ref[...].astype(F32) + ob_ref[...].astype(F32)
    heads = []
    for h in range(GLA_HEADS):
        o_h = o[:, h * GLA_DV:(h + 1) * GLA_DV]
        heads.append(o_h * lax.rsqrt(jnp.mean(o_h * o_h, axis=-1, keepdims=True) + EPS))
    gate = bm_ref[:, B_G:B_G + GLA_WIDTH].astype(F32)
    return jnp.concatenate(heads, axis=-1) * gn_ref[...] * jax.nn.silu(gate)


def _spatial_gating(bm_ref, lng_ref, lnb_ref, wst_ref, sb_ref):
    tm = bm_ref.shape[0]
    gu = jax.nn.gelu(bm_ref[:, B_U:B_U + SGU_WIDTH].astype(F32))
    gv = jax.nn.gelu(bm_ref[:, B_VS:B_VS + SGU_WIDTH].astype(F32))
    xc = gv - jnp.mean(gv, axis=-1, keepdims=True)
    v_ln = xc * lax.rsqrt(jnp.mean(xc * xc, axis=-1, keepdims=True) + EPS) * lng_ref[...] + lnb_ref[...]
    v_b = v_ln.astype(BF16)
    head_of_lane = lax.broadcasted_iota(jnp.int32, (SGU_CHUNK, SGU_WIDTH), 1) // SGU_HEAD_DIM
    parts = []
    for n in range(tm // SGU_CHUNK):
        rows = slice(n * SGU_CHUNK, (n + 1) * SGU_CHUNK)
        all_heads = _dot(wst_ref[...], v_b[rows])
        mixed = jnp.zeros((SGU_CHUNK, SGU_WIDTH), F32)
        for h in range(SGU_HEADS):
            mixed = mixed + jnp.where(head_of_lane == h, all_heads[h * SGU_CHUNK:(h + 1) * SGU_CHUNK], 0.0)
        parts.append(gu[rows] * (mixed + sb_ref[...]))
    return jnp.concatenate(parts, axis=0)


def _pool_mixer(xp_ref, hp_ref, hn_ref, pw_ref, ps_ref, s0, seq_len):
    tm = xp_ref.shape[0]
    xp = xp_ref[...]
    ext = jnp.concatenate(
        [jnp.where(s0 > 0, hp_ref[...], 0.0), xp, jnp.where(s0 + tm < seq_len, hn_ref[...], 0.0)], axis=0)
    rows_ext = tm + 2 * HALO
    back = lambda t, d: pltpu.roll(t, d, 0)
    ahead = lambda t, d: pltpu.roll(t, rows_ext - d, 0)
    core = lambda t: t[HALO:HALO + tm]
    lo, hi = ext[:, :V7X_LANES], ext[:, V7X_LANES:]
    lo2, hi2 = lo + back(lo, 1), hi + back(hi, 1)
    lo4, hi4 = back(lo2, 1) + ahead(lo2, 1), back(hi2, 1) + ahead(hi2, 1)
    hi8 = back(hi4, 2) + ahead(hi4, 2)
    hi16 = back(hi8, 4) + ahead(hi8, 4)
    first_group = lax.broadcasted_iota(jnp.int32, (tm, V7X_LANES), 1) < POOL_GROUP_DIM
    win = jnp.concatenate([jnp.where(first_group, core(lo2), core(lo4)),
                           jnp.where(first_group, core(hi8), core(hi16))], axis=1)

    def half_window(rows):
        group = lax.broadcasted_iota(jnp.int32, (rows, POOL_WIDTH), 1) // POOL_GROUP_DIM
        return jnp.where(group == 0, 1, jnp.where(group == 1, 2, jnp.where(group == 2, 4, 8)))

    def edge_mean(r0):
        half = half_window(HALO)
        t = s0 + r0 + lax.broadcasted_iota(jnp.int32, (HALO, POOL_WIDTH), 0)
        cnt = jnp.minimum(t + half, seq_len) - jnp.maximum(t - half, 0)
        return win[r0:r0 + HALO] / cnt.astype(F32)

    inv_width = 0.5 / half_window(1).astype(F32)
    pooled = jnp.concatenate(
        [edge_mean(0), win[HALO:tm - HALO] * inv_width, edge_mean(tm - HALO)], axis=0)
    return _dot((pooled - xp).astype(BF16), pw_ref[...]) * ps_ref[...]


def _mix_kernel(of_ref, ob_ref, bm_ref, xp_ref, hp_ref, hn_ref, x_ref, gn_ref, lng_ref, lnb_ref,
                wst_ref, sb_ref, pw_ref, ps_ref, wo_ref, out_ref, mixed_ref, *,
                seq_len, tiles_per_seq, n_tiles):
    tm = x_ref.shape[0]
    t = pl.program_id(0)

    @pl.when(t == 0)
    def _():
        mixed_ref[...] = jnp.zeros_like(mixed_ref)

    fill = lax.rem(t, 2)
    use = 1 - fill
    s0 = lax.rem(jnp.minimum(t, n_tiles - 1), tiles_per_seq) * tm

    n_groups = 4
    gw = D_MODEL // n_groups

    def project(g):
        cols = slice(g * gw, (g + 1) * gw)
        out_ref[:, cols] = x_ref[:, cols] + _dot(mixed_ref[use], wo_ref[:, cols])

    project(0)
    a_out = _gla_gated(of_ref, ob_ref, bm_ref, gn_ref)
    project(1)
    b_out = _spatial_gating(bm_ref, lng_ref, lnb_ref, wst_ref, sb_ref)
    project(2)
    c_out = _pool_mixer(xp_ref, hp_ref, hn_ref, pw_ref, ps_ref, s0, seq_len)
    project(3)
    mixed_ref[fill] = jnp.concatenate([a_out, b_out, c_out], axis=-1).astype(BF16)


def _mix(o_f, o_b, bm, xp, x, p, layer):
    bsz, seq, _ = x.shape
    tm = TM_MIX
    n = seq // tm
    total = bsz * n
    hb = tm // HALO
    n_halo_blocks = seq // HALO

    def mixed_tile(t):
        u = jnp.minimum(t, total - 1)
        return u // n, lax.rem(u, n)

    def projected_tile(t):
        u = jnp.maximum(t - 1, 0)
        return u // n, lax.rem(u, n)

    def tile(which):
        return lambda t: (which(t)[0], which(t)[1], 0)

    def halo_before(t):
        b, j = mixed_tile(t)
        return b, jnp.maximum(j * hb - 1, 0), 0

    def halo_after(t):
        b, j = mixed_tile(t)
        return b, jnp.minimum((j + 1) * hb, n_halo_blocks - 1), 0

    layer_block = lambda t: (layer, 0, 0)
    return pl.pallas_call(
        functools.partial(_mix_kernel, seq_len=seq, tiles_per_seq=n, n_tiles=total),
        grid=(total + 1,),
        in_specs=[
            pl.BlockSpec((None, tm, GLA_WIDTH), tile(mixed_tile)),
            pl.BlockSpec((None, tm, GLA_WIDTH), tile(mixed_tile)),
            pl.BlockSpec((None, tm, B_WIDTH), tile(mixed_tile)),
            pl.BlockSpec((None, tm, POOL_WIDTH), tile(mixed_tile)),
            pl.BlockSpec((None, HALO, POOL_WIDTH), halo_before),
            pl.BlockSpec((None, HALO, POOL_WIDTH), halo_after),
            pl.BlockSpec((None, tm, D_MODEL), tile(projected_tile)),
            pl.BlockSpec((None, 1, GLA_WIDTH), layer_block),
            pl.BlockSpec((None, 1, SGU_WIDTH), layer_block),
            pl.BlockSpec((None, 1, SGU_WIDTH), layer_block),
            pl.BlockSpec((None, SGU_HEADS * SGU_CHUNK, SGU_CHUNK), layer_block),
            pl.BlockSpec((None, SGU_CHUNK, SGU_WIDTH), layer_block),
            pl.BlockSpec((None, POOL_WIDTH, POOL_WIDTH), layer_block),
            pl.BlockSpec((None, 1, POOL_WIDTH), layer_block),
            pl.BlockSpec((None, D_MODEL, D_MODEL), layer_block),
        ],
        out_specs=pl.BlockSpec((None, tm, D_MODEL), tile(projected_tile)),
        out_shape=jax.ShapeDtypeStruct((bsz, seq, D_MODEL), F32),
        scratch_shapes=[pltpu.VMEM((2, tm, D_MODEL), BF16)],
        name="mix_out",
        compiler_params=pltpu.CompilerParams(
            dimension_semantics=("arbitrary",),
            vmem_limit_bytes=VMEM_LIMIT_MIX),
    )(o_f, o_b, bm, xp, xp, xp, x, p["gla_norm_g"], p["sgu_ln_g"], p["sgu_ln_b"], p["sgu_w"],
      p["sgu_b"], p["pool_w"], p["pool_scale"], p["w_o"])


def _ffn_kernel(x_ref, hp_ref, hn_ref, g_ref, w1_ref, cw_ref, cb_ref, w2_ref, nf_ref, out_ref,
                hid_ref, *, seq_len, final):
    tm = x_ref.shape[0]
    _, prev_ok, next_ok = _halo_valid(tm, seq_len)
    x = x_ref[...]
    x_ext = jnp.concatenate(
        [jnp.where(prev_ok, hp_ref[...], 0.0), x, jnp.where(next_ok, hn_ref[...], 0.0)], axis=0)
    h_ext = _rmsnorm(x_ext, g_ref[...])
    hb_ext = h_ext.astype(BF16)
    hb = h_ext[HALO:HALO + tm].astype(BF16)
    rows_ext = tm + 2 * HALO
    for c0 in range(0, D_FF, FFN_COL_CHUNK):
        cols = slice(c0, c0 + FFN_COL_CHUNK)
        a = _dot(hb_ext, w1_ref[:, cols])
        up = _dot(hb, w1_ref[:, D_FF + c0:D_FF + c0 + FFN_COL_CHUNK])
        conv = (pltpu.roll(a, 1, 0) * cw_ref[0:1, cols] + a * cw_ref[1:2, cols]
                + pltpu.roll(a, rows_ext - 1, 0) * cw_ref[2:3, cols] + cb_ref[:, cols])
        hid_ref[:, cols] = (jax.nn.silu(conv[HALO:HALO + tm]) * up).astype(BF16)
    y = x + _dot(hid_ref[...], w2_ref[...])
    if final:
        y = _rmsnorm(y, nf_ref[...])
    out_ref[...] = y


def _ffn(x, p, layer, final):
    bsz, seq, _ = x.shape
    tm = TM_FFN
    hb = tm // HALO
    n_halo_blocks = seq // HALO
    tile = lambda b, j: (b, j, 0)
    layer_block = lambda b, j: (layer, 0, 0)
    return pl.pallas_call(
        functools.partial(_ffn_kernel, seq_len=seq, final=final),
        grid=(bsz, seq // tm),
        in_specs=[
            pl.BlockSpec((None, tm, D_MODEL), tile),
            pl.BlockSpec((None, HALO, D_MODEL), lambda b, j: (b, jnp.maximum(j * hb - 1, 0), 0)),
            pl.BlockSpec((None, HALO, D_MODEL),
                         lambda b, j: (b, jnp.minimum((j + 1) * hb, n_halo_blocks - 1), 0)),
            pl.BlockSpec((None, 1, D_MODEL), layer_block),
            pl.BlockSpec((None, D_MODEL, 2 * D_FF), layer_block, pipeline_mode=pl.Buffered(1)),
            pl.BlockSpec((None, 3, D_FF), layer_block),
            pl.BlockSpec((None, 1, D_FF), layer_block),
            pl.BlockSpec((None, D_FF, D_MODEL), layer_block, pipeline_mode=pl.Buffered(1)),
            pl.BlockSpec((1, D_MODEL), lambda b, j: (0, 0)),
        ],
        out_specs=pl.BlockSpec((None, tm, D_MODEL), tile),
        out_shape=jax.ShapeDtypeStruct((bsz, seq, D_MODEL), F32),
        scratch_shapes=[pltpu.VMEM((tm, D_FF), BF16)],
        name="conv_glu_ffn",
        compiler_params=pltpu.CompilerParams(
            dimension_semantics=("arbitrary", "arbitrary"),
            vmem_limit_bytes=VMEM_LIMIT_FFN),
    )(x, x, x, p["norm2_g"], p["w_ffn_in"], p["conv_w"], p["conv_b"], p["w_ffn_out"], p["norm_f"])


def _prepare_params(norm1_g, w_in, gla_gate_w2, gla_gate_b, gla_norm_g, sgu_ln_g, sgu_ln_b,
                    sgu_w, sgu_b, pool_w, pool_scale, w_o, norm2_g, w_ffn_in, conv_w, conv_b,
                    w_ffn_out, norm_f):
    depth = w_in.shape[0]
    row = lambda t: t.reshape(depth, 1, t.shape[-1])
    c_qkv = 2 * GLA_KWIDTH + GLA_WIDTH
    c_g = c_qkv + GLA_WIDTH
    c_lr = c_g + 2 * GLA_GATE_RANK
    w_in_b = w_in.astype(BF16)
    w_in_p = jnp.concatenate([
        w_in_b[:, :, :c_qkv], w_in_b[:, :, c_g:c_lr],
        jnp.zeros((depth, D_MODEL, LR_PAD - 2 * GLA_GATE_RANK), BF16),
        w_in_b[:, :, c_qkv:c_g], w_in_b[:, :, c_lr:]], axis=-1)
    w2cat = jnp.zeros((depth, LR_PAD, 2 * GLA_KWIDTH), F32)
    for d in range(2):
        w2cat = w2cat.at[:, d * GLA_GATE_RANK:(d + 1) * GLA_GATE_RANK,
                         d * GLA_KWIDTH:(d + 1) * GLA_KWIDTH].set(gla_gate_w2[:, d])
    pool_bd = jnp.einsum("lgcd,gh->lgchd", pool_w, jnp.eye(len(POOL_WINDOWS), dtype=pool_w.dtype))
    return {
        "norm1_g": row(norm1_g),
        "w_in": w_in_p,
        "gate_w2": w2cat.astype(BF16),
        "gate_b": gla_gate_b.reshape(depth, 1, 2 * GLA_KWIDTH),
        "gla_norm_g": row(gla_norm_g),
        "sgu_ln_g": row(sgu_ln_g),
        "sgu_ln_b": row(sgu_ln_b),
        "sgu_w": sgu_w.reshape(depth, SGU_HEADS * SGU_CHUNK, SGU_CHUNK).astype(BF16),
        "sgu_b": jnp.repeat(jnp.swapaxes(sgu_b, 1, 2), SGU_HEAD_DIM, axis=2),
        "pool_w": pool_bd.reshape(depth, POOL_WIDTH, POOL_WIDTH).astype(BF16),
        "pool_scale": row(pool_scale),
        "w_o": w_o.astype(BF16),
        "norm2_g": row(norm2_g),
        "w_ffn_in": w_ffn_in.astype(BF16),
        "conv_w": conv_w,
        "conv_b": row(conv_b),
        "w_ffn_out": w_ffn_out.astype(BF16),
        "norm_f": norm_f.reshape(1, D_MODEL),
    }


def _trunk(x, p):
    depth = p["w_in"].shape[0]
    for layer in range(depth):
        qks_f, qks_b, v, dec, bm, xp = _inproj(x, p, layer)
        o_f, o_b = _gla(qks_f, qks_b, v, dec, layer)
        x = _mix(o_f, o_b, bm, xp, x, p, layer)
        x = _ffn(x, p, layer, final=(layer == depth - 1))
    return x


def kernel(x_prompt, x_sample, norm1_g, w_in, gla_gate_w2, gla_gate_b, gla_norm_g, sgu_ln_g, sgu_ln_b, sgu_w, sgu_b, pool_w, pool_scale, w_o, norm2_g, w_ffn_in, conv_w, conv_b, w_ffn_out, norm_f):
    p = _prepare_params(norm1_g, w_in, gla_gate_w2, gla_gate_b, gla_norm_g, sgu_ln_g, sgu_ln_b,
                        sgu_w, sgu_b, pool_w, pool_scale, w_o, norm2_g, w_ffn_in, conv_w, conv_b,
                        w_ffn_out, norm_f)
    return (_trunk(x_prompt, p), _trunk(x_sample, p))
```

```python
import functools

import jax
import jax.numpy as jnp
from jax import lax
from jax.experimental import pallas as pl
from jax.experimental.pallas import tpu as pltpu

F32 = jnp.float32
BF16 = jnp.bfloat16

D_MODEL = 1024
GLA_HEADS = 4
GLA_WIDTH = 512
GLA_DV = 128
GLA_DK = 64
GLA_KWIDTH = 256
GLA_GATE_RANK = 16
GLA_GATE_TAU = 16.0
GLA_CHUNK = 64
SGU_HEADS = 4
SGU_WIDTH = 256
SGU_HEAD_DIM = 64
SGU_CHUNK = 128
POOL_WINDOWS = (2, 4, 8, 16)
POOL_WIDTH = 256
POOL_GROUP_DIM = 64
POOL_HALO = max(POOL_WINDOWS) // 2
D_FF = 2816
EPS = 1e-6

V7X_LANES = 128
V7X_SUBLANES = 8
V7X_VMEM_BYTES = 64 * 1024 * 1024

LR_PAD = V7X_LANES
A_WIDTH = 2 * GLA_KWIDTH + GLA_WIDTH + LR_PAD
B_WIDTH = GLA_WIDTH + 2 * SGU_WIDTH
A_Q, A_K, A_V, A_LR = 0, GLA_KWIDTH, 2 * GLA_KWIDTH, 2 * GLA_KWIDTH + GLA_WIDTH
B_G, B_U, B_VS = 0, GLA_WIDTH, GLA_WIDTH + SGU_WIDTH
IN_WIDTH_PADDED = A_WIDTH + B_WIDTH + POOL_WIDTH

HALO = V7X_SUBLANES
assert POOL_HALO <= HALO

TM_PROJ = 1024
TB_GLA = 2048
TM_MIX = 1024
TM_FFN = 1024
FFN_COL_CHUNK = 256
PROJ_ROW_BLOCK = 256

VMEM_LIMIT_PROJ = 48 * 1024 * 1024
VMEM_LIMIT_GLA = 56 * 1024 * 1024
VMEM_LIMIT_MIX = 48 * 1024 * 1024
VMEM_LIMIT_FFN = 56 * 1024 * 1024
assert max(VMEM_LIMIT_PROJ, VMEM_LIMIT_GLA, VMEM_LIMIT_MIX, VMEM_LIMIT_FFN) < V7X_VMEM_BYTES


def _rmsnorm(x, g):
    return x * lax.rsqrt(jnp.mean(x * x, axis=-1, keepdims=True) + EPS) * g


def _dot(a, b):
    return jnp.dot(a, b, preferred_element_type=F32)


def _dot_nt(a, b):
    return lax.dot_general(a, b, (((1,), (1,)), ((), ())), preferred_element_type=F32)


def _dot_tn(a, b):
    return lax.dot_general(a, b, (((0,), (0,)), ((), ())), preferred_element_type=F32)


def _log_sigmoid(z):
    return jnp.minimum(z, 0.0) - jnp.log(1.0 + jnp.exp(-jnp.abs(z)))


def _chunk_prefix_sums(x):
    row_in_chunk = lax.broadcasted_iota(jnp.int32, x.shape, 0) % GLA_CHUNK
    shift = 1
    while shift < GLA_CHUNK:
        x = x + jnp.where(row_in_chunk >= shift, pltpu.roll(x, shift, 0), 0.0)
        shift *= 2
    return x


def _gate_stages(a_parts, w2_ref, b2_ref, qks_f_ref, qks_b_ref, dec_ref, r0, rb):
    rows = pl.ds(r0, rb)
    n_chunks = rb // GLA_CHUNK
    fwd, bwd = slice(0, GLA_KWIDTH), slice(GLA_KWIDTH, 2 * GLA_KWIDTH)
    val = {}

    def gates():
        z = _dot(a_parts["lr"].astype(BF16), w2_ref[...]) + b2_ref[...]
        val["log_a"] = _log_sigmoid(z) * (1.0 / GLA_GATE_TAU)

    def scan():
        val["prefix"] = _chunk_prefix_sums(val["log_a"])

    def totals():
        prefix = val["prefix"]
        last = [prefix[(c + 1) * GLA_CHUNK - 1:(c + 1) * GLA_CHUNK, :] for c in range(n_chunks)]
        val["total"] = jnp.concatenate(
            [jnp.broadcast_to(t, (GLA_CHUNK, 2 * GLA_KWIDTH)) for t in last], axis=0)
        dec_ref[pl.ds(r0 // GLA_CHUNK * HALO, n_chunks * HALO)] = jnp.concatenate(
            [jnp.broadcast_to(jnp.exp(t), (HALO, 2 * GLA_KWIDTH)) for t in last], axis=0)

    def emit(out_ref, g_cum, g_total):
        if "q" not in val:
            val["q"] = a_parts["q"] * (GLA_DK ** -0.5)
        q, k = val["q"], a_parts["k"]
        out_ref[rows, 0:GLA_KWIDTH] = (q * jnp.exp(g_cum)).astype(BF16)
        out_ref[rows, GLA_KWIDTH:2 * GLA_KWIDTH] = (k * jnp.exp(-g_cum)).astype(BF16)
        out_ref[rows, 2 * GLA_KWIDTH:] = (k * jnp.exp(g_total - g_cum)).astype(BF16)

    def forward():
        emit(qks_f_ref, val["prefix"][:, fwd], val["total"][:, fwd])

    def backward():
        suffix = val["total"][:, bwd] - val["prefix"][:, bwd] + val["log_a"][:, bwd]
        emit(qks_b_ref, suffix, val["total"][:, bwd])

    return [gates, scan, totals, forward, backward]


def _inproj_kernel(x_ref, g_ref, w_ref, w2_ref, b2_ref, qks_f_ref, qks_b_ref, v_ref, dec_ref, b_ref,
                   xp_ref):
    tm = x_ref.shape[0]
    rb = PROJ_ROW_BLOCK
    n_blocks = tm // rb
    hb = [None] * n_blocks
    a_parts = [dict() for _ in range(n_blocks)]

    def normalize(r):
        hb[r] = _rmsnorm(x_ref[pl.ds(r * rb, rb)], g_ref[...]).astype(BF16)

    def project(r, c0, c1):
        return _dot(hb[r], w_ref[:, c0:c1])

    def gla_pieces(r):
        rows = pl.ds(r * rb, rb)

        def part(name, c0, c1):
            return lambda: a_parts[r].__setitem__(name, project(r, c0, c1))

        def values(c0, c1):
            def piece():
                v_ref[rows, c0 - A_V:c1 - A_V] = project(r, c0, c1).astype(BF16)
            return piece

        half_v = A_V + GLA_WIDTH // 2
        return [part("lr", A_LR, A_LR + LR_PAD), part("q", A_Q, A_K), part("k", A_K, A_V),
                values(A_V, half_v), values(half_v, A_LR)]

    def other_pieces(r):
        rows = pl.ds(r * rb, rb)

        def branch(c0, c1):
            def piece():
                b_ref[rows, c0:c1] = project(r, A_WIDTH + c0, A_WIDTH + c1).astype(BF16)
            return piece

        def pool():
            xp_ref[rows] = project(r, A_WIDTH + B_WIDTH, IN_WIDTH_PADDED)

        step = 2 * V7X_LANES
        return [branch(c, c + step) for c in range(0, B_WIDTH, step)] + [pool]

    normalize(0)
    for piece in gla_pieces(0):
        piece()
    for r in range(1, n_blocks):
        normalize(r)
    queue = []
    for r in range(n_blocks):
        if r + 1 < n_blocks:
            queue = gla_pieces(r + 1) + queue
        queue = queue + other_pieces(r)
        for stage in _gate_stages(a_parts[r], w2_ref, b2_ref, qks_f_ref, qks_b_ref, dec_ref, r * rb, rb):
            stage()
            for _ in range(2):
                if queue:
                    queue.pop(0)()
    for piece in queue:
        piece()


def _inproj(x, p, layer):
    bsz, seq, _ = x.shape
    tm = TM_PROJ
    tile = lambda b, j: (b, j, 0)
    layer_block = lambda b, j: (layer, 0, 0)
    return pl.pallas_call(
        _inproj_kernel,
        grid=(bsz, seq // tm),
        in_specs=[
            pl.BlockSpec((None, tm, D_MODEL), tile),
            pl.BlockSpec((None, 1, D_MODEL), layer_block),
            pl.BlockSpec((None, D_MODEL, IN_WIDTH_PADDED), layer_block),
            pl.BlockSpec((None, LR_PAD, 2 * GLA_KWIDTH), layer_block),
            pl.BlockSpec((None, 1, 2 * GLA_KWIDTH), layer_block),
        ],
        out_specs=[
            pl.BlockSpec((None, tm, 3 * GLA_KWIDTH), tile),
            pl.BlockSpec((None, tm, 3 * GLA_KWIDTH), tile),
            pl.BlockSpec((None, tm, GLA_WIDTH), tile),
            pl.BlockSpec((None, tm // GLA_CHUNK * HALO, 2 * GLA_KWIDTH), tile),
            pl.BlockSpec((None, tm, B_WIDTH), tile),
            pl.BlockSpec((None, tm, POOL_WIDTH), tile),
        ],
        out_shape=[
            jax.ShapeDtypeStruct((bsz, seq, 3 * GLA_KWIDTH), BF16),
            jax.ShapeDtypeStruct((bsz, seq, 3 * GLA_KWIDTH), BF16),
            jax.ShapeDtypeStruct((bsz, seq, GLA_WIDTH), BF16),
            jax.ShapeDtypeStruct((bsz, seq // GLA_CHUNK * HALO, 2 * GLA_KWIDTH), F32),
            jax.ShapeDtypeStruct((bsz, seq, B_WIDTH), BF16),
            jax.ShapeDtypeStruct((bsz, seq, POOL_WIDTH), F32),
        ],
        name="inproj",
        compiler_params=pltpu.CompilerParams(
            dimension_semantics=("arbitrary", "arbitrary"),
            vmem_limit_bytes=VMEM_LIMIT_PROJ),
    )(x, p["norm1_g"], p["w_in"], p["gate_w2"], p["gate_b"])


def _gla_chunk_terms(qks_ref, v_ref, dec_ref, lanes, c, reverse):
    rows = pl.ds(c * GLA_CHUNK, GLA_CHUNK)
    q_c = qks_ref[rows, 0:GLA_KWIDTH]
    k_c = qks_ref[rows, GLA_KWIDTH:2 * GLA_KWIDTH]
    ks_c = qks_ref[rows, 2 * GLA_KWIDTH:3 * GLA_KWIDTH]
    v_c = v_ref[rows]
    zero_b = jnp.zeros((), BF16)
    k_head = lax.broadcasted_iota(jnp.int32, (GLA_CHUNK, GLA_KWIDTH), 1) // GLA_DK
    v_head = lax.broadcasted_iota(jnp.int32, (GLA_CHUNK, GLA_WIDTH), 1) // GLA_DV
    k_bd = jnp.concatenate([jnp.where(k_head == h, k_c, zero_b) for h in range(GLA_HEADS)], axis=0)
    v_bd = jnp.concatenate([jnp.where(v_head == h, v_c, zero_b) for h in range(GLA_HEADS)], axis=0)
    i = lax.broadcasted_iota(jnp.int32, (GLA_CHUNK, GLA_HEADS * GLA_CHUNK), 0)
    j = lax.broadcasted_iota(jnp.int32, (GLA_CHUNK, GLA_HEADS * GLA_CHUNK), 1) % GLA_CHUNK
    keep = (j >= i) if reverse else (j <= i)
    att = jnp.where(keep, _dot_nt(q_c, k_bd), 0.0).astype(BF16)
    kv = []
    for pair in range(GLA_HEADS // 2):
        both = _dot_tn(ks_c[:, pair * 2 * GLA_DK:(pair + 1) * 2 * GLA_DK],
                       v_c[:, pair * 2 * GLA_DV:(pair + 1) * 2 * GLA_DV])
        kv.append(both[:GLA_DK, :GLA_DV])
        kv.append(both[GLA_DK:, GLA_DV:])
    dec_row = dec_ref[c * HALO:c * HALO + 1, lanes]
    dec = jnp.transpose(jnp.broadcast_to(dec_row, (GLA_DV, GLA_KWIDTH)))
    return {"q": q_c, "att": att, "v_bd": v_bd, "kv": kv, "dec": dec}


def _gla_chunk_step(terms, state, o_ref, c):
    zero_blk = jnp.zeros((GLA_DK, GLA_DV), BF16)
    rows = slice(c * GLA_CHUNK, (c + 1) * GLA_CHUNK)
    for pair in range(GLA_HEADS // 2):
        h0, h1 = 2 * pair, 2 * pair + 1
        j_lanes = slice(pair * 2 * GLA_CHUNK, (pair + 1) * 2 * GLA_CHUNK)
        k_lanes = slice(pair * 2 * GLA_DK, (pair + 1) * 2 * GLA_DK)
        v_lanes = slice(pair * 2 * GLA_DV, (pair + 1) * 2 * GLA_DV)
        s_bd = jnp.concatenate(
            [jnp.concatenate([state[h0].astype(BF16), zero_blk], axis=1),
             jnp.concatenate([zero_blk, state[h1].astype(BF16)], axis=1)], axis=0)
        lhs = jnp.concatenate([terms["att"][:, j_lanes], terms["q"][:, k_lanes]], axis=1)
        rhs = jnp.concatenate([terms["v_bd"][j_lanes, v_lanes], s_bd], axis=0)
        o_ref[rows, v_lanes] = _dot(lhs, rhs).astype(o_ref.dtype)
    dec = terms["dec"]
    return [dec[h * GLA_DK:(h + 1) * GLA_DK] * state[h] + terms["kv"][h] for h in range(GLA_HEADS)]


def _gla_kernel(qks_f, v_f, dec_f, qks_b, v_b, dec_b, of_ref, ob_ref, st_f, st_b):
    @pl.when(pl.program_id(1) == 0)
    def _():
        st_f[...] = jnp.zeros_like(st_f)
        st_b[...] = jnp.zeros_like(st_b)

    n_chunks = qks_f.shape[0] // GLA_CHUNK
    dirs = (
        dict(qks=qks_f, v=v_f, dec=dec_f, lanes=slice(0, GLA_KWIDTH), o=of_ref, st=st_f, reverse=False),
        dict(qks=qks_b, v=v_b, dec=dec_b, lanes=slice(GLA_KWIDTH, 2 * GLA_KWIDTH), o=ob_ref, st=st_b,
             reverse=True),
    )
    terms = [[_gla_chunk_terms(d["qks"], d["v"], d["dec"], d["lanes"], c, d["reverse"])
              for c in range(n_chunks)] for d in dirs]
    states = [[d["st"][h] for h in range(GLA_HEADS)] for d in dirs]
    for step in range(n_chunks):
        for n, d in enumerate(dirs):
            c = n_chunks - 1 - step if d["reverse"] else step
            states[n] = _gla_chunk_step(terms[n][c], states[n], d["o"], c)
    for n, d in enumerate(dirs):
        for h in range(GLA_HEADS):
            d["st"][h] = states[n][h]


def _gla(qks_f, qks_b, v, dec, layer):
    bsz, seq, _ = v.shape
    tb = TB_GLA
    n = seq // tb
    dec_rows = tb // GLA_CHUNK * HALO
    fwd = lambda b, i: (b, i, 0)
    bwd = lambda b, i: (b, n - 1 - i, 0)
    return pl.pallas_call(
        _gla_kernel,
        grid=(bsz, n),
        in_specs=[
            pl.BlockSpec((None, tb, 3 * GLA_KWIDTH), fwd),
            pl.BlockSpec((None, tb, GLA_WIDTH), fwd),
            pl.BlockSpec((None, dec_rows, 2 * GLA_KWIDTH), fwd),
            pl.BlockSpec((None, tb, 3 * GLA_KWIDTH), bwd),
            pl.BlockSpec((None, tb, GLA_WIDTH), bwd),
            pl.BlockSpec((None, dec_rows, 2 * GLA_KWIDTH), bwd),
        ],
        out_specs=[
            pl.BlockSpec((None, tb, GLA_WIDTH), fwd),
            pl.BlockSpec((None, tb, GLA_WIDTH), bwd),
        ],
        out_shape=[jax.ShapeDtypeStruct((bsz, seq, GLA_WIDTH), BF16)] * 2,
        scratch_shapes=[pltpu.VMEM((GLA_HEADS, GLA_DK, GLA_DV), F32)] * 2,
        name="gla_scan",
        compiler_params=pltpu.CompilerParams(
            dimension_semantics=("arbitrary", "arbitrary"),
            vmem_limit_bytes=VMEM_LIMIT_GLA),
    )(qks_f, v, dec, qks_b, v, dec)


def _halo_valid(tm, seq_len):
    s0 = pl.program_id(1) * tm
    return s0, s0 > 0, s0 + tm < seq_len


def _gla_gated(of_ref, ob_ref, bm_ref, gn_ref, rows):
    o = of_ref[rows].astype(F32) + ob_ref[rows].astype(F32)
    heads = []
    for h in range(GLA_HEADS):
        o_h = o[:, h * GLA_DV:(h + 1) * GLA_DV]
        heads.append(o_h * lax.rsqrt(jnp.mean(o_h * o_h, axis=-1, keepdims=True) + EPS))
    gate = bm_ref[rows, B_G:B_G + GLA_WIDTH].astype(F32)
    return jnp.concatenate(heads, axis=-1) * gn_ref[...] * jax.nn.silu(gate)


def _spatial_gating(bm_ref, lng_ref, lnb_ref, wst_ref, sb_ref, rows):
    gu = jax.nn.gelu(bm_ref[rows, B_U:B_U + SGU_WIDTH].astype(F32))
    gv = jax.nn.gelu(bm_ref[rows, B_VS:B_VS + SGU_WIDTH].astype(F32))
    xc = gv - jnp.mean(gv, axis=-1, keepdims=True)
    v_ln = xc * lax.rsqrt(jnp.mean(xc * xc, axis=-1, keepdims=True) + EPS) * lng_ref[...] + lnb_ref[...]
    head_of_lane = lax.broadcasted_iota(jnp.int32, (SGU_CHUNK, SGU_WIDTH), 1) // SGU_HEAD_DIM
    all_heads = _dot(wst_ref[...], v_ln.astype(BF16))
    mixed = jnp.zeros((SGU_CHUNK, SGU_WIDTH), F32)
    for h in range(SGU_HEADS):
        mixed = mixed + jnp.where(head_of_lane == h, all_heads[h * SGU_CHUNK:(h + 1) * SGU_CHUNK], 0.0)
    return gu * (mixed + sb_ref[...])


def _pool_mixer(xp_ref, hp_ref, hn_ref, pw_ref, ps_ref, s0, seq_len):
    tm = xp_ref.shape[0]
    xp = xp_ref[...]
    ext = jnp.concatenate(
        [jnp.where(s0 > 0, hp_ref[...], 0.0), xp, jnp.where(s0 + tm < seq_len, hn_ref[...], 0.0)], axis=0)
    rows_ext = tm + 2 * HALO
    back = lambda t, d: pltpu.roll(t, d, 0)
    ahead = lambda t, d: pltpu.roll(t, rows_ext - d, 0)
    core = lambda t: t[HALO:HALO + tm]
    lo, hi = ext[:, :V7X_LANES], ext[:, V7X_LANES:]
    lo2, hi2 = lo + back(lo, 1), hi + back(hi, 1)
    lo4, hi4 = back(lo2, 1) + ahead(lo2, 1), back(hi2, 1) + ahead(hi2, 1)
    hi8 = back(hi4, 2) + ahead(hi4, 2)
    hi16 = back(hi8, 4) + ahead(hi8, 4)
    first_group = lax.broadcasted_iota(jnp.int32, (tm, V7X_LANES), 1) < POOL_GROUP_DIM
    win = jnp.concatenate([jnp.where(first_group, core(lo2), core(lo4)),
                           jnp.where(first_group, core(hi8), core(hi16))], axis=1)

    def half_window(rows):
        group = lax.broadcasted_iota(jnp.int32, (rows, POOL_WIDTH), 1) // POOL_GROUP_DIM
        return jnp.where(group == 0, 1, jnp.where(group == 1, 2, jnp.where(group == 2, 4, 8)))

    def edge_mean(r0):
        half = half_window(HALO)
        t = s0 + r0 + lax.broadcasted_iota(jnp.int32, (HALO, POOL_WIDTH), 0)
        cnt = jnp.minimum(t + half, seq_len) - jnp.maximum(t - half, 0)
        return win[r0:r0 + HALO] / cnt.astype(F32)

    inv_width = 0.5 / half_window(1).astype(F32)
    pooled = jnp.concatenate(
        [edge_mean(0), win[HALO:tm - HALO] * inv_width, edge_mean(tm - HALO)], axis=0)
    return _dot((pooled - xp).astype(BF16), pw_ref[...]) * ps_ref[...]


def _mix_kernel(of_ref, ob_ref, bm_ref, xp_ref, hp_ref, hn_ref, x_ref, gn_ref, lng_ref, lnb_ref,
                wst_ref, sb_ref, pw_ref, ps_ref, wo_ref, out_ref, mixed_ref, *,
                seq_len, tiles_per_seq, n_tiles):
    tm = x_ref.shape[0]
    t = pl.program_id(0)

    @pl.when(t == 0)
    def _():
        mixed_ref[...] = jnp.zeros_like(mixed_ref)

    fill = lax.rem(t, 2)
    use = 1 - fill
    s0 = lax.rem(jnp.minimum(t, n_tiles - 1), tiles_per_seq) * tm

    n_groups = 4
    gw = D_MODEL // n_groups

    def project(g):
        cols = slice(g * gw, (g + 1) * gw)
        out_ref[:, cols] = x_ref[:, cols] + _dot(mixed_ref[use], wo_ref[:, cols])

    n_chunks = tm // SGU_CHUNK
    pool_lanes = slice(GLA_WIDTH + SGU_WIDTH, D_MODEL)
    for n in range(n_chunks):
        if n % (n_chunks // n_groups) == 0:
            project(n // (n_chunks // n_groups))
        rows = pl.ds(n * SGU_CHUNK, SGU_CHUNK)
        a_out = _gla_gated(of_ref, ob_ref, bm_ref, gn_ref, rows)
        b_out = _spatial_gating(bm_ref, lng_ref, lnb_ref, wst_ref, sb_ref, rows)
        mixed_ref[fill, rows, 0:GLA_WIDTH + SGU_WIDTH] = jnp.concatenate(
            [a_out, b_out], axis=-1).astype(BF16)
    c_out = _pool_mixer(xp_ref, hp_ref, hn_ref, pw_ref, ps_ref, s0, seq_len)
    mixed_ref[fill, :, pool_lanes] = c_out.astype(BF16)


def _mix(o_f, o_b, bm, xp, x, p, layer):
    bsz, seq, _ = x.shape
    tm = TM_MIX
    n = seq // tm
    total = bsz * n
    hb = tm // HALO
    n_halo_blocks = seq // HALO

    def mixed_tile(t):
        u = jnp.minimum(t, total - 1)
        return u // n, lax.rem(u, n)

    def projected_tile(t):
        u = jnp.maximum(t - 1, 0)
        return u // n, lax.rem(u, n)

    def tile(which):
        return lambda t: (which(t)[0], which(t)[1], 0)

    def halo_before(t):
        b, j = mixed_tile(t)
        return b, jnp.maximum(j * hb - 1, 0), 0

    def halo_after(t):
        b, j = mixed_tile(t)
        return b, jnp.minimum((j + 1) * hb, n_halo_blocks - 1), 0

    layer_block = lambda t: (layer, 0, 0)
    return pl.pallas_call(
        functools.partial(_mix_kernel, seq_len=seq, tiles_per_seq=n, n_tiles=total),
        grid=(total + 1,),
        in_specs=[
            pl.BlockSpec((None, tm, GLA_WIDTH), tile(mixed_tile)),
            pl.BlockSpec((None, tm, GLA_WIDTH), tile(mixed_tile)),
            pl.BlockSpec((None, tm, B_WIDTH), tile(mixed_tile)),
            pl.BlockSpec((None, tm, POOL_WIDTH), tile(mixed_tile)),
            pl.BlockSpec((None, HALO, POOL_WIDTH), halo_before),
            pl.BlockSpec((None, HALO, POOL_WIDTH), halo_after),
            pl.BlockSpec((None, tm, D_MODEL), tile(projected_tile)),
            pl.BlockSpec((None, 1, GLA_WIDTH), layer_block),
            pl.BlockSpec((None, 1, SGU_WIDTH), layer_block),
            pl.BlockSpec((None, 1, SGU_WIDTH), layer_block),
            pl.BlockSpec((None, SGU_HEADS * SGU_CHUNK, SGU_CHUNK), layer_block),
            pl.BlockSpec((None, SGU_CHUNK, SGU_WIDTH), layer_block),
            pl.BlockSpec((None, POOL_WIDTH, POOL_WIDTH), layer_block),
            pl.BlockSpec((None, 1, POOL_WIDTH), layer_block),
            pl.BlockSpec((None, D_MODEL, D_MODEL), layer_block),
        ],
        out_specs=pl.BlockSpec((None, tm, D_MODEL), tile(projected_tile)),
        out_shape=jax.ShapeDtypeStruct((bsz, seq, D_MODEL), F32),
        scratch_shapes=[pltpu.VMEM((2, tm, D_MODEL), BF16)],
        name="mix_out",
        compiler_params=pltpu.CompilerParams(
            dimension_semantics=("arbitrary",),
            vmem_limit_bytes=VMEM_LIMIT_MIX),
    )(o_f, o_b, bm, xp, xp, xp, x, p["gla_norm_g"], p["sgu_ln_g"], p["sgu_ln_b"], p["sgu_w"],
      p["sgu_b"], p["pool_w"], p["pool_scale"], p["w_o"])


def _ffn_kernel(x_ref, hp_ref, hn_ref, g_ref, w1_ref, cw_ref, cb_ref, w2_ref, nf_ref, out_ref,
                hid_ref, *, seq_len, final):
    tm = x_ref.shape[0]
    _, prev_ok, next_ok = _halo_valid(tm, seq_len)
    x = x_ref[...]
    x_ext = jnp.concatenate(
        [jnp.where(prev_ok, hp_ref[...], 0.0), x, jnp.where(next_ok, hn_ref[...], 0.0)], axis=0)
    h_ext = _rmsnorm(x_ext, g_ref[...])
    hb_ext = h_ext.astype(BF16)
    hb = h_ext[HALO:HALO + tm].astype(BF16)
    rows_ext = tm + 2 * HALO
    for c0 in range(0, D_FF, FFN_COL_CHUNK):
        cols = slice(c0, c0 + FFN_COL_CHUNK)
        a = _dot(hb_ext, w1_ref[:, cols])
        up = _dot(hb, w1_ref[:, D_FF + c0:D_FF + c0 + FFN_COL_CHUNK])
        conv = (pltpu.roll(a, 1, 0) * cw_ref[0:1, cols] + a * cw_ref[1:2, cols]
                + pltpu.roll(a, rows_ext - 1, 0) * cw_ref[2:3, cols] + cb_ref[:, cols])
        hid_ref[:, cols] = (jax.nn.silu(conv[HALO:HALO + tm]) * up).astype(BF16)
    y = x + _dot(hid_ref[...], w2_ref[...])
    if final:
        y = _rmsnorm(y, nf_ref[...])
    out_ref[...] = y


def _ffn(x, p, layer, final):
    bsz, seq, _ = x.shape
    tm = TM_FFN
    hb = tm // HALO
    n_halo_blocks = seq // HALO
    tile = lambda b, j: (b, j, 0)
    layer_block = lambda b, j: (layer, 0, 0)
    return pl.pallas_call(
        functools.partial(_ffn_kernel, seq_len=seq, final=final),
        grid=(bsz, seq // tm),
        in_specs=[
            pl.BlockSpec((None, tm, D_MODEL), tile),
            pl.BlockSpec((None, HALO, D_MODEL), lambda b, j: (b, jnp.maximum(j * hb - 1, 0), 0)),
            pl.BlockSpec((None, HALO, D_MODEL),
                         lambda b, j: (b, jnp.minimum((j + 1) * hb, n_halo_blocks - 1), 0)),
            pl.BlockSpec((None, 1, D_MODEL), layer_block),
            pl.BlockSpec((None, D_MODEL, 2 * D_FF), layer_block, pipeline_mode=pl.Buffered(1)),
            pl.BlockSpec((None, 3, D_FF), layer_block),
            pl.BlockSpec((None, 1, D_FF), layer_block),
            pl.BlockSpec((None, D_FF, D_MODEL), layer_block, pipeline_mode=pl.Buffered(1)),
            pl.BlockSpec((1, D_MODEL), lambda b, j: (0, 0)),
        ],
        out_specs=pl.BlockSpec((None, tm, D_MODEL), tile),
        out_shape=jax.ShapeDtypeStruct((bsz, seq, D_MODEL), F32),
        scratch_shapes=[pltpu.VMEM((tm, D_FF), BF16)],
        name="conv_glu_ffn",
        compiler_params=pltpu.CompilerParams(
            dimension_semantics=("arbitrary", "arbitrary"),
            vmem_limit_bytes=VMEM_LIMIT_FFN),
    )(x, x, x, p["norm2_g"], p["w_ffn_in"], p["conv_w"], p["conv_b"], p["w_ffn_out"], p["norm_f"])


def _prepare_params(norm1_g, w_in, gla_gate_w2, gla_gate_b, gla_norm_g, sgu_ln_g, sgu_ln_b,
                    sgu_w, sgu_b, pool_w, pool_scale, w_o, norm2_g, w_ffn_in, conv_w, conv_b,
                    w_ffn_out, norm_f):
    depth = w_in.shape[0]
    row = lambda t: t.reshape(depth, 1, t.shape[-1])
    c_qkv = 2 * GLA_KWIDTH + GLA_WIDTH
    c_g = c_qkv + GLA_WIDTH
    c_lr = c_g + 2 * GLA_GATE_RANK
    w_in_b = w_in.astype(BF16)
    w_in_p = jnp.concatenate([
        w_in_b[:, :, :c_qkv], w_in_b[:, :, c_g:c_lr],
        jnp.zeros((depth, D_MODEL, LR_PAD - 2 * GLA_GATE_RANK), BF16),
        w_in_b[:, :, c_qkv:c_g], w_in_b[:, :, c_lr:]], axis=-1)
    w2cat = jnp.zeros((depth, LR_PAD, 2 * GLA_KWIDTH), F32)
    for d in range(2):
        w2cat = w2cat.at[:, d * GLA_GATE_RANK:(d + 1) * GLA_GATE_RANK,
                         d * GLA_KWIDTH:(d + 1) * GLA_KWIDTH].set(gla_gate_w2[:, d])
    pool_bd = jnp.einsum("lgcd,gh->lgchd", pool_w, jnp.eye(len(POOL_WINDOWS), dtype=pool_w.dtype))
    return {
        "norm1_g": row(norm1_g),
        "w_in": w_in_p,
        "gate_w2": w2cat.astype(BF16),
        "gate_b": gla_gate_b.reshape(depth, 1, 2 * GLA_KWIDTH),
        "gla_norm_g": row(gla_norm_g),
        "sgu_ln_g": row(sgu_ln_g),
        "sgu_ln_b": row(sgu_ln_b),
        "sgu_w": sgu_w.reshape(depth, SGU_HEADS * SGU_CHUNK, SGU_CHUNK).astype(BF16),
        "sgu_b": jnp.repeat(jnp.swapaxes(sgu_b, 1, 2), SGU_HEAD_DIM, axis=2),
        "pool_w": pool_bd.reshape(depth, POOL_WIDTH, POOL_WIDTH).astype(BF16),
        "pool_scale": row(pool_scale),
        "w_o": w_o.astype(BF16),
        "norm2_g": row(norm2_g),
        "w_ffn_in": w_ffn_in.astype(BF16),
        "conv_w": conv_w,
        "conv_b": row(conv_b),
        "w_ffn_out": w_ffn_out.astype(BF16),
        "norm_f": norm_f.reshape(1, D_MODEL),
    }


def _trunk(x, p):
    depth = p["w_in"].shape[0]
    for layer in range(depth):
        qks_f, qks_b, v, dec, bm, xp = _inproj(x, p, layer)
        o_f, o_b = _gla(qks_f, qks_b, v, dec, layer)
        x = _mix(o_f, o_b, bm, xp, x, p, layer)
        x = _ffn(x, p, layer, final=(layer == depth - 1))
    return x


def kernel(x_prompt, x_sample, norm1_g, w_in, gla_gate_w2, gla_gate_b, gla_norm_g, sgu_ln_g, sgu_ln_b, sgu_w, sgu_b, pool_w, pool_scale, w_o, norm2_g, w_ffn_in, conv_w, conv_b, w_ffn_out, norm_f):
    p = _prepare_params(norm1_g, w_in, gla_gate_w2, gla_gate_b, gla_norm_g, sgu_ln_g, sgu_ln_b,
                        sgu_w, sgu_b, pool_w, pool_scale, w_o, norm2_g, w_ffn_in, conv_w, conv_b,
                        w_ffn_out, norm_f)
    return (_trunk(x_prompt, p), _trunk(x_sample, p))
```
